```python
import jax, jax.numpy as jnp
from jax import lax
import numpy as np

D_MODEL = 1024
BATCH = 8
SEQ = 2048
DEPTH = 4
DEC_BATCH = 32
DEC_SEQ = 4
PAST_LEN = 8192
PAGE_SIZE = 128

MIX_WIDTH = D_MODEL
ATTN_WIDTH = MIX_WIDTH // 2
POOL_WIDTH = MIX_WIDTH - ATTN_WIDTH
HEAD_DIM = 64
N_HEADS = ATTN_WIDTH // HEAD_DIM
MOBA_BLOCK = 256
MOBA_TOPK = 3
QUERY_BLOCK = 128
POOL_WINDOWS = (2, 4, 8, 16)
N_POOL_GROUPS = len(POOL_WINDOWS)
POOL_GROUP = POOL_WIDTH // N_POOL_GROUPS
POOL_BUF = max(POOL_WINDOWS) - 1
D_FF = 2816
CONV_WIDTH = 3
RMS_EPS = 1e-6
F32 = jnp.float32

kernel_name = "hymba_pool_moba_convffn_step"


def rmsnorm(x, g):
    x32 = x.astype(F32)
    y = x32 * lax.rsqrt(jnp.mean(x32 * x32, axis=-1, keepdims=True) + RMS_EPS)
    return (y * g.astype(F32)).astype(x.dtype)


def alibi_slopes():
    return jnp.exp2(-8.0 * jnp.arange(1, N_HEADS + 1, dtype=F32) / N_HEADS)


def moba_attend(q, q_pos, k_blocks, v_blocks, k_means, slopes):
    nb = k_blocks.shape[2]
    k_sel = min(MOBA_TOPK, nb)
    n_past = q_pos // MOBA_BLOCK
    gate = jnp.einsum('bhqd,bhnd->bhqn', q.astype(F32), k_means)
    fully_past = jnp.arange(nb)[None, :] < n_past[:, None]
    gate = jnp.where(fully_past, gate, -jnp.inf)
    _, top_idx = lax.top_k(gate, k_sel)
    own = jnp.broadcast_to(n_past[None, None, :, None], top_idx.shape[:3] + (1,)).astype(top_idx.dtype)
    idx = jnp.concatenate([top_idx, own], axis=-1)
    gather = jax.vmap(jax.vmap(lambda blk, ix: blk[ix]))
    kg = gather(k_blocks, idx)
    vg = gather(v_blocks, idx)
    k_pos = idx[..., None] * MOBA_BLOCK + jnp.arange(MOBA_BLOCK, dtype=idx.dtype)
    chosen = jnp.concatenate([jnp.arange(k_sel)[None, :] < n_past[:, None],
                              jnp.ones((q_pos.shape[0], 1), dtype=bool)], axis=-1)
    dist = q_pos[None, None, :, None, None] - k_pos
    mask = chosen[None, None, :, :, None] & (dist >= 0)
    s = (jnp.einsum('bhqd,bhqjsd->bhqjs', q, kg, preferred_element_type=F32) * (HEAD_DIM ** -0.5)
         - slopes[None, :, None, None, None] * dist.astype(F32))
    s = jnp.where(mask, s, -jnp.inf)
    b, h, tq, j, sb = s.shape
    p = jax.nn.softmax(s.reshape(b, h, tq, j * sb), axis=-1).reshape(s.shape)
    return jnp.einsum('bhqjs,bhqjsd->bhqd', p.astype(vg.dtype), vg)


def moba_mixer(q, k, v, past_k, past_v, pos0, slopes):
    B, T = q.shape[0], q.shape[1]
    if past_k is not None:
        k_all = jnp.concatenate([past_k.astype(k.dtype), k], axis=1)
        v_all = jnp.concatenate([past_v.astype(v.dtype), v], axis=1)
    else:
        k_all, v_all = k, v
    L = k_all.shape[1]
    nb = -(-L // MOBA_BLOCK)
    pad = nb * MOBA_BLOCK - L
    k_all = jnp.pad(k_all, ((0, 0), (0, pad), (0, 0), (0, 0)))
    v_all = jnp.pad(v_all, ((0, 0), (0, pad), (0, 0), (0, 0)))
    kb = k_all.reshape(B, nb, MOBA_BLOCK, N_HEADS, HEAD_DIM).transpose(0, 3, 1, 2, 4)
    vb = v_all.reshape(B, nb, MOBA_BLOCK, N_HEADS, HEAD_DIM).transpose(0, 3, 1, 2, 4)
    k_means = kb.astype(F32).mean(axis=3)
    qh = q.transpose(0, 2, 1, 3)
    q_pos = pos0 + jnp.arange(T, dtype=jnp.int32)
    if T > QUERY_BLOCK and T % QUERY_BLOCK == 0:
        nqc = T // QUERY_BLOCK
        qc = qh.reshape(B, N_HEADS, nqc, QUERY_BLOCK, HEAD_DIM).transpose(2, 0, 1, 3, 4)
        pc = q_pos.reshape(nqc, QUERY_BLOCK)
        out = lax.map(lambda a: moba_attend(a[0], a[1], kb, vb, k_means, slopes), (qc, pc))
        out = out.transpose(1, 2, 0, 3, 4).reshape(B, N_HEADS, T, HEAD_DIM)
    else:
        out = moba_attend(qh, q_pos, kb, vb, k_means, slopes)
    return out.transpose(0, 2, 1, 3).reshape(B, T, ATTN_WIDTH)


def pool_mixer(u, prefix, pos0, w_pool, pool_scale):
    T = u.shape[1]
    ext_raw = jnp.concatenate([prefix.astype(u.dtype), u], axis=1)
    ext = ext_raw.astype(F32)
    c = jnp.concatenate([jnp.zeros_like(ext[:, :1]), jnp.cumsum(ext, axis=1)], axis=1)
    u32 = u.astype(F32)
    pos = pos0 + jnp.arange(T, dtype=jnp.int32)
    outs = []
    for g, w in enumerate(POOL_WINDOWS):
        sl = slice(g * POOL_GROUP, (g + 1) * POOL_GROUP)
        hi = c[:, POOL_BUF + 1:POOL_BUF + T + 1, sl]
        lo = c[:, POOL_BUF + 1 - w:POOL_BUF + T + 1 - w, sl]
        cnt = jnp.minimum(w, pos + 1).astype(F32)
        pooled = (hi - lo) / cnt[None, :, None] - u32[:, :, sl]
        outs.append(jnp.einsum('btc,cd->btd', pooled.astype(u.dtype), w_pool[g]))
    out = jnp.concatenate(outs, axis=-1) * pool_scale
    return out, ext_raw[:, T:]


def conv_ffn(h, prefix, w_up, conv_w, conv_b, w_down):
    T = h.shape[1]
    up = jnp.einsum('btd,df->btf', h, w_up)
    ext = jnp.concatenate([prefix.astype(up.dtype), up], axis=1)
    conv = conv_b
    for j in range(CONV_WIDTH):
        conv = conv + conv_w[j] * ext[:, j:j + T]
    a, g = jnp.split(conv, 2, axis=-1)
    out = jnp.einsum('btf,fd->btd', jax.nn.silu(g) * a, w_down)
    return out, ext[:, T:]


def layer(x, past_k, past_v, pool_prefix, conv_prefix, pos0, slopes,
          attn_norm, w_in, q_norm, k_norm, w_pool, pool_scale, w_out,
          ffn_norm, w_up, conv_w, conv_b, w_down):
    B, T, _ = x.shape
    h = rmsnorm(x, attn_norm)
    proj = jnp.einsum('btd,de->bte', h, w_in)
    q, k, v, u = jnp.split(proj, [ATTN_WIDTH, 2 * ATTN_WIDTH, 3 * ATTN_WIDTH], axis=-1)
    q = rmsnorm(q.reshape(B, T, N_HEADS, HEAD_DIM), q_norm)
    k = rmsnorm(k.reshape(B, T, N_HEADS, HEAD_DIM), k_norm)
    v = v.reshape(B, T, N_HEADS, HEAD_DIM)
    attn = moba_mixer(q, k, v, past_k, past_v, pos0, slopes)
    pool, pool_state = pool_mixer(u, pool_prefix, pos0, w_pool, pool_scale)
    x = x + jnp.einsum('bte,ed->btd', jnp.concatenate([attn, pool], axis=-1), w_out)
    h2 = rmsnorm(x, ffn_norm)
    ffn, conv_state = conv_ffn(h2, conv_prefix, w_up, conv_w, conv_b, w_down)
    x = x + ffn
    return x, k, v, pool_state, conv_state


def setup_inputs(seed: int = 0) -> dict:
    key = jax.random.key(seed)
    ks = jax.random.split(key, 20)
    n_pages = PAST_LEN // PAGE_SIZE
    n_phys = (DEC_BATCH * n_pages * 5) // 4
    nrm = jax.random.normal
    x_prompt = nrm(ks[0], (BATCH, SEQ, D_MODEL), F32)
    x_sample = nrm(ks[1], (DEC_BATCH, DEC_SEQ, D_MODEL), F32)
    cache_k = nrm(ks[2], (DEPTH, n_phys, PAGE_SIZE, N_HEADS, HEAD_DIM), F32)
    cache_v = nrm(ks[3], (DEPTH, n_phys, PAGE_SIZE, N_HEADS, HEAD_DIM), F32)
    state_pool = nrm(ks[4], (DEPTH, DEC_BATCH, POOL_BUF, POOL_WIDTH), F32)
    state_conv = 0.5 * nrm(ks[5], (DEPTH, DEC_BATCH, CONV_WIDTH - 1, 2 * D_FF), F32)
    page_table = jax.random.permutation(ks[6], n_phys)[:DEC_BATCH * n_pages].reshape(DEC_BATCH, n_pages).astype(jnp.int32)
    attn_norm = 1.0 + 0.01 * nrm(ks[7], (DEPTH, D_MODEL), F32)
    w_in = nrm(ks[8], (DEPTH, D_MODEL, 3 * ATTN_WIDTH + POOL_WIDTH), F32) * D_MODEL ** -0.5
    q_norm = 1.0 + 0.01 * nrm(ks[9], (DEPTH, HEAD_DIM), F32)
    k_norm = 1.0 + 0.01 * nrm(ks[10], (DEPTH, HEAD_DIM), F32)
    w_pool = nrm(ks[11], (DEPTH, N_POOL_GROUPS, POOL_GROUP, POOL_GROUP), F32) * POOL_GROUP ** -0.5
    pool_scale = 1.0 + 0.1 * nrm(ks[12], (DEPTH, POOL_WIDTH), F32)
    w_out = nrm(ks[13], (DEPTH, MIX_WIDTH, D_MODEL), F32) * MIX_WIDTH ** -0.5
    ffn_norm = 1.0 + 0.01 * nrm(ks[14], (DEPTH, D_MODEL), F32)
    w_up = nrm(ks[15], (DEPTH, D_MODEL, 2 * D_FF), F32) * D_MODEL ** -0.5
    conv_w = nrm(ks[16], (DEPTH, CONV_WIDTH, 2 * D_FF), F32) * CONV_WIDTH ** -0.5
    conv_b = 0.01 * nrm(ks[17], (DEPTH, 2 * D_FF), F32)
    w_down = nrm(ks[18], (DEPTH, D_FF, D_MODEL), F32) * D_FF ** -0.5
    return {"x_prompt": x_prompt, "x_sample": x_sample, "cache_k": cache_k, "cache_v": cache_v,
            "state_pool": state_pool, "state_conv": state_conv, "page_table": page_table,
            "attn_norm": attn_norm, "w_in": w_in, "q_norm": q_norm, "k_norm": k_norm,
            "w_pool": w_pool, "pool_scale": pool_scale, "w_out": w_out, "ffn_norm": ffn_norm,
            "w_up": w_up, "conv_w": conv_w, "conv_b": conv_b, "w_down": w_down}


def reference(x_prompt, x_sample, cache_k, cache_v, state_pool, state_conv, page_table,
              attn_norm, w_in, q_norm, k_norm, w_pool, pool_scale, w_out,
              ffn_norm, w_up, conv_w, conv_b, w_down):
    slopes = alibi_slopes()
    n_pages = PAST_LEN // PAGE_SIZE
    bp = x_prompt.shape[0]
    bs = x_sample.shape[0]
    y_p, y_s = x_prompt, x_sample
    kp_l, vp_l, pp_l, cp_l, ks_l, vs_l, ps_l, cs_l = [], [], [], [], [], [], [], []
    for l in range(DEPTH):
        params = (attn_norm[l], w_in[l], q_norm[l], k_norm[l], w_pool[l], pool_scale[l], w_out[l],
                  ffn_norm[l], w_up[l], conv_w[l], conv_b[l], w_down[l])
        pool0 = jnp.zeros((bp, POOL_BUF, POOL_WIDTH), x_prompt.dtype)
        conv0 = jnp.zeros((bp, CONV_WIDTH - 1, 2 * D_FF), x_prompt.dtype)
        y_p, kp, vp, pp, cp = layer(y_p, None, None, pool0, conv0, 0, slopes, *params)
        past_k = cache_k[l][page_table].reshape(bs, n_pages * PAGE_SIZE, N_HEADS, HEAD_DIM)
        past_v = cache_v[l][page_table].reshape(bs, n_pages * PAGE_SIZE, N_HEADS, HEAD_DIM)
        y_s, ks_, vs_, ps_, cs_ = layer(y_s, past_k, past_v, state_pool[l], state_conv[l], PAST_LEN, slopes, *params)
        kp_l.append(kp); vp_l.append(vp); pp_l.append(pp); cp_l.append(cp)
        ks_l.append(ks_); vs_l.append(vs_); ps_l.append(ps_); cs_l.append(cs_)
    new_k_prompt = jnp.stack(kp_l)
    new_v_prompt = jnp.stack(vp_l)
    new_pool_prompt = jnp.stack(pp_l)
    new_conv_prompt = jnp.stack(cp_l)
    new_k_sample = jnp.stack(ks_l)
    new_v_sample = jnp.stack(vs_l)
    new_pool_sample = jnp.stack(ps_l)
    new_conv_sample = jnp.stack(cs_l)
    return (y_p, y_s, new_k_prompt, new_v_prompt, new_pool_prompt, new_conv_prompt,
            new_k_sample, new_v_sample, new_pool_sample, new_conv_sample)
```

```python
import functools

import numpy as np
import jax
import jax.numpy as jnp
from jax import lax
from jax.experimental import pallas as pl
from jax.experimental.pallas import tpu as pltpu

F32 = jnp.float32
BF16 = jnp.bfloat16
I32 = jnp.int32

D_MODEL = 1024
HEAD_DIM = 64
N_HEADS = 8
ATTN_WIDTH = N_HEADS * HEAD_DIM
POOL_WIDTH = 512
PROJ_WIDTH = 3 * ATTN_WIDTH + POOL_WIDTH
MOBA_BLOCK = 256
MOBA_TOPK = 3
POOL_WINDOWS = (2, 4, 8, 16)
POOL_GROUP = 128
POOL_BUF = 15
D_FF = 2816
CONV_WIDTH = 3
RMS_EPS = 1e-6
PAGE_SIZE = 128
PAGES_PER_BLOCK = MOBA_BLOCK // PAGE_SIZE

LANES = 128
SUBLANES = 8
FF_CHUNK = 256
N_FF_CHUNKS = D_FF // FF_CHUNK
POOL_CARRY = 16
NEG_BIG = -1e30
QK_SCALE = HEAD_DIM ** -0.5
DEC_RING = 6
VMEM_LIMIT = 56 * 1024 * 1024

_NT = (((1,), (1,)), ((), ()))


def _rms(x, gain):
    ms = jnp.mean(x * x, axis=-1, keepdims=True)
    return x * lax.rsqrt(ms + RMS_EPS) * gain


def _inproj_kernel(x_ref, g_ref, w_ref, qkg_ref, gsum_ref, q_ref, k_ref, v_ref, u_ref):
    h = _rms(x_ref[...], g_ref[...]).astype(BF16)
    proj = jnp.dot(h, w_ref[...], preferred_element_type=F32)
    gsum = gsum_ref[...]
    for j in range(2 * ATTN_WIDTH // LANES):
        blk = proj[:, j * LANES:(j + 1) * LANES]
        sq = blk * blk
        hi = sq.astype(BF16)
        lo = (sq - hi.astype(F32)).astype(BF16)
        ssum = jnp.dot(jnp.concatenate([hi, lo], axis=1), gsum, preferred_element_type=F32)
        y = blk * lax.rsqrt(ssum * (1.0 / HEAD_DIM) + RMS_EPS) * qkg_ref[:, j * LANES:(j + 1) * LANES]
        dst = q_ref if j < ATTN_WIDTH // LANES else k_ref
        c0 = (j % (ATTN_WIDTH // LANES)) * LANES
        dst[:, c0:c0 + LANES] = y
    v_ref[...] = proj[:, 2 * ATTN_WIDTH:3 * ATTN_WIDTH]
    u_ref[...] = proj[:, 3 * ATTN_WIDTH:]


def _inproj(x, layer, attn_norm, w_in, qk_gain, gsum, tm):
    n = x.shape[0]
    out = jax.ShapeDtypeStruct((n, ATTN_WIDTH), F32)
    row = lambda i: (i, 0)
    return pl.pallas_call(
        _inproj_kernel,
        grid=(n // tm,),
        in_specs=[
            pl.BlockSpec((tm, D_MODEL), row),
            pl.BlockSpec((None, 1, D_MODEL), lambda i: (layer, 0, 0)),
            pl.BlockSpec((None, D_MODEL, PROJ_WIDTH), lambda i: (layer, 0, 0), pipeline_mode=pl.Buffered(1)),
            pl.BlockSpec((None, 1, 2 * ATTN_WIDTH), lambda i: (layer, 0, 0)),
            pl.BlockSpec((2 * LANES, LANES), lambda i: (0, 0)),
        ],
        out_specs=[pl.BlockSpec((tm, ATTN_WIDTH), row)] * 4,
        out_shape=[out] * 4,
        compiler_params=pltpu.CompilerParams(dimension_semantics=("arbitrary",), vmem_limit_bytes=VMEM_LIMIT),
        name="inproj",
    )(x, attn_norm, w_in, qk_gain, gsum)


def _prompt_attn_kernel(slopes_ref, q_ref, k_ref, v_ref, o_ref, qa_ref, ka_ref, vb_ref):
    t_len = q_ref.shape[1]
    nb = t_len // MOBA_BLOCK
    pair = pl.program_id(1)
    q = q_ref[0]
    k = k_ref[0]
    lane = lax.broadcasted_iota(I32, (1, LANES), 1)
    in_h0 = lane < HEAD_DIM

    kmeans = jnp.sum(k.reshape(nb, MOBA_BLOCK, LANES), axis=1) * (1.0 / MOBA_BLOCK)
    km2 = jnp.concatenate([jnp.where(in_h0, kmeans, 0.0), jnp.where(in_h0, 0.0, kmeans)], axis=0)
    gate = lax.dot_general(km2, q, _NT, precision=lax.Precision.HIGHEST,
                           preferred_element_type=F32)

    q_blk = lax.broadcasted_iota(I32, (nb, t_len), 1) // MOBA_BLOCK
    n_idx = lax.broadcasted_iota(I32, (nb, t_len), 0)
    aux_rows = []
    for hh in range(2):
        g = gate[hh * nb:(hh + 1) * nb]
        slope = slopes_ref[2 * pair + hh]
        rank = jnp.zeros((nb, t_len), F32)
        for m in range(nb):
            gm = g[m:m + 1, :]
            beats = jnp.where(gm > g, 1.0, jnp.where(gm == g, jnp.where(m < n_idx, 1.0, 0.0), 0.0))
            rank = rank + jnp.where(m < q_blk, beats, 0.0)
        thresh = jnp.where(n_idx < q_blk, float(MOBA_TOPK), jnp.where(n_idx == q_blk, float(nb + 1), -1.0))
        bias = jnp.where(rank < thresh, slope * (MOBA_BLOCK * n_idx).astype(F32), NEG_BIG)
        slope_rows = jnp.where(n_idx == 0, slope, 0.0)
        aux_rows.append(jnp.concatenate([bias, slope_rows], axis=0))
    pad = jnp.zeros((HEAD_DIM - 2 * nb, t_len), F32)
    aux_t = jnp.concatenate([aux_rows[1], pad, aux_rows[0], pad], axis=0).T

    pos = lax.broadcasted_iota(I32, (t_len, LANES), 0)
    la = lax.broadcasted_iota(I32, (t_len, LANES), 1) % HEAD_DIM
    k_aux = jnp.where(la < nb, jnp.where(pos // MOBA_BLOCK == la, 1.0, 0.0),
                      jnp.where(la == nb, (pos % MOBA_BLOCK).astype(F32), 0.0))
    qs = q * QK_SCALE
    qa_ref[0] = jnp.where(in_h0, qs, aux_t).astype(BF16)
    qa_ref[1] = jnp.where(in_h0, aux_t, qs).astype(BF16)
    ka_ref[0] = jnp.where(in_h0, k, k_aux).astype(BF16)
    ka_ref[1] = jnp.where(in_h0, k_aux, k).astype(BF16)
    vb_ref[...] = v_ref[0].astype(BF16)

    r_i = lax.broadcasted_iota(I32, (MOBA_BLOCK, MOBA_BLOCK), 0)
    c_i = lax.broadcasted_iota(I32, (MOBA_BLOCK, MOBA_BLOCK), 1)
    causal = c_i <= r_i
    for i in range(nb):
        r0 = i * MOBA_BLOCK
        n_keys = r0 + MOBA_BLOCK
        outs = []
        for hh in range(2):
            s = lax.dot_general(qa_ref[hh, r0:n_keys, :], ka_ref[hh, 0:n_keys, :], _NT,
                                preferred_element_type=F32)
            s_diag = jnp.where(causal, s[:, r0:], NEG_BIG)
            m = jnp.max(s_diag, axis=1, keepdims=True)
            if i > 0:
                s_past = s[:, :r0]
                m = jnp.maximum(m, jnp.max(s_past, axis=1, keepdims=True))
                p = jnp.concatenate([jnp.exp(s_past - m), jnp.exp(s_diag - m)], axis=1)
            else:
                p = jnp.exp(s_diag - m)
            l = jnp.sum(p, axis=1, keepdims=True)
            o = jnp.dot(p.astype(BF16), vb_ref[0:n_keys, :], preferred_element_type=F32)
            outs.append(o / l)
        o_ref[0, r0:n_keys, :] = jnp.where(in_h0, outs[0], outs[1]).astype(o_ref.dtype)


def _prompt_attn(slopes, q, k, v):
    b, t_len, _ = q.shape
    blk = pl.BlockSpec((1, t_len, LANES), lambda i, j: (i, 0, j))
    return pl.pallas_call(
        _prompt_attn_kernel,
        grid=(b, ATTN_WIDTH // LANES),
        in_specs=[pl.BlockSpec(memory_space=pltpu.SMEM), blk, blk, blk],
        out_specs=blk,
        out_shape=jax.ShapeDtypeStruct((b, t_len, ATTN_WIDTH), BF16),
        scratch_shapes=[pltpu.VMEM((2, t_len, LANES), BF16), pltpu.VMEM((2, t_len, LANES), BF16),
                        pltpu.VMEM((t_len, LANES), BF16)],
        compiler_params=pltpu.CompilerParams(dimension_semantics=("arbitrary", "arbitrary"),
                                             vmem_limit_bytes=VMEM_LIMIT),
        name="prompt_attn",
    )(slopes, q, k, v)


def _decode_attn_kernel(pt_ref, slope_ref, qrep_ref, knew_ref, vnew_ref, ck_ref, cv_ref, o_ref,
                        buf_ref, sem, s_ref, sums_ref, *, past_len, n_new):
    n_seq = qrep_ref.shape[0]
    n_rows = qrep_ref.shape[1]
    n_blk = past_len // MOBA_BLOCK
    per_seq = 2 * n_blk
    n_chunks = n_seq * per_seq
    n_pad = knew_ref.shape[1]

    def copies(src_ref, page0, page1, slot):
        return [pltpu.make_async_copy(src_ref.at[page], buf_ref.at[slot, pl.ds(h * PAGE_SIZE, PAGE_SIZE)],
                                      sem.at[slot]) for h, page in enumerate((page0, page1))]

    def start_chunk(g):
        seq = g // per_seq
        c = g % per_seq
        n = c % n_blk
        slot = g % DEC_RING
        page0 = pt_ref[seq, PAGES_PER_BLOCK * n]
        page1 = pt_ref[seq, PAGES_PER_BLOCK * n + 1]

        @pl.when(c < n_blk)
        def _():
            for cp in copies(ck_ref, page0, page1, slot):
                cp.start()

        @pl.when(c >= n_blk)
        def _():
            for cp in copies(cv_ref, page0, page1, slot):
                cp.start()

    def wait_chunk(slot):
        for cp in copies(ck_ref, 0, 0, slot):
            cp.wait()

    def advance(g):
        slot = g % DEC_RING
        wait_chunk(slot)
        nxt = g + DEC_RING - 1

        @pl.when(nxt < n_chunks)
        def _():
            start_chunk(nxt)
        return slot

    for g0 in range(DEC_RING - 1):
        start_chunk(jnp.int32(g0))

    row = lax.broadcasted_iota(I32, (n_rows, 1), 0)
    h_row = row % N_HEADS
    t_row = row // N_HEADS
    head_mask = (lax.broadcasted_iota(I32, (1, ATTN_WIDTH), 1) // HEAD_DIM) == h_row
    slope = slope_ref[...]
    key_lane = lax.broadcasted_iota(I32, (1, MOBA_BLOCK), 1)
    alibi0 = -(slope * (past_len + t_row - key_lane).astype(F32))
    blk_idx = lax.broadcasted_iota(I32, (n_rows, n_blk), 1)
    new_idx = lax.broadcasted_iota(I32, (1, n_pad), 1)
    own_ok = jnp.logical_and(new_idx <= t_row, new_idx < n_new)
    own_alibi = -(slope * (t_row - new_idx).astype(F32))

    def seq_body(seq, carry):
        qf = jnp.where(head_mask, qrep_ref[seq], 0.0)
        qb = (qf * QK_SCALE).astype(BF16)

        def k_body(n, c):
            slot = advance(seq * per_seq + n)
            kf = buf_ref[slot]
            sums_ref[pl.ds(n, 1), :] = jnp.sum(kf, axis=0, keepdims=True)
            s_ref[n] = lax.dot_general(qb, kf.astype(BF16), _NT, preferred_element_type=F32)
            return c
        lax.fori_loop(0, n_blk, k_body, 0)

        means = sums_ref[...] * (1.0 / MOBA_BLOCK)
        gate = lax.dot_general(qf, means, _NT, precision=lax.Precision.HIGHEST,
                               preferred_element_type=F32)
        sel = jnp.zeros((n_rows, n_blk), F32)
        for _ in range(min(MOBA_TOPK, n_blk)):
            mx = jnp.max(gate, axis=1, keepdims=True)
            first = jnp.min(jnp.where(gate == mx, blk_idx, n_blk), axis=1, keepdims=True)
            pick = blk_idx == first
            sel = jnp.where(pick, 1.0, sel)
            gate = jnp.where(pick, -jnp.inf, gate)

        m_run = jnp.full((n_rows, MOBA_BLOCK), NEG_BIG, F32)
        for n in range(n_blk):
            sb = jnp.where(sel[:, n:n + 1] > 0.0, s_ref[n] + (alibi0 + slope * float(MOBA_BLOCK * n)), NEG_BIG)
            s_ref[n] = sb
            m_run = jnp.maximum(m_run, sb)

        s_own = lax.dot_general(qb, knew_ref[seq].astype(BF16), _NT, preferred_element_type=F32)
        s_own = jnp.where(own_ok, s_own + own_alibi, NEG_BIG)
        m = jnp.maximum(jnp.max(m_run, axis=1, keepdims=True), jnp.max(s_own, axis=1, keepdims=True))
        p_own = jnp.exp(s_own - m)
        l0 = jnp.sum(p_own, axis=1, keepdims=True)
        acc0 = jnp.dot(p_own.astype(BF16), vnew_ref[seq].astype(BF16), preferred_element_type=F32)

        def v_body(n, c):
            acc, l = c
            slot = advance(seq * per_seq + n_blk + n)
            p = jnp.exp(s_ref[n] - m)
            l = l + jnp.sum(p, axis=1, keepdims=True)
            acc = acc + jnp.dot(p.astype(BF16), buf_ref[slot].astype(BF16), preferred_element_type=F32)
            return acc, l
        acc, l = lax.fori_loop(0, n_blk, v_body, (acc0, l0))

        o = jnp.where(head_mask, acc / l, 0.0)
        o_ref[seq] = jnp.sum(o.reshape(n_rows // N_HEADS, N_HEADS, ATTN_WIDTH), axis=1)
        return carry
    lax.fori_loop(0, n_seq, seq_body, 0)


def _decode_attn(page_ids, slope_rows, q_rep, k_new, v_new, cache_k, cache_v, past_len, n_new):
    n_seq, n_rows, _ = q_rep.shape
    n_blk = past_len // MOBA_BLOCK
    vmem = pl.BlockSpec(memory_space=pltpu.VMEM)
    return pl.pallas_call(
        functools.partial(_decode_attn_kernel, past_len=past_len, n_new=n_new),
        in_specs=[pl.BlockSpec(memory_space=pltpu.SMEM), vmem, vmem, vmem, vmem,
                  pl.BlockSpec(memory_space=pl.ANY), pl.BlockSpec(memory_space=pl.ANY)],
        out_specs=vmem,
        out_shape=jax.ShapeDtypeStruct((n_seq, n_rows // N_HEADS, ATTN_WIDTH), F32),
        scratch_shapes=[pltpu.VMEM((DEC_RING, MOBA_BLOCK, ATTN_WIDTH), F32),
                        pltpu.SemaphoreType.DMA((DEC_RING,)),
                        pltpu.VMEM((n_blk, n_rows, MOBA_BLOCK), F32),
                        pltpu.VMEM((n_blk, ATTN_WIDTH), F32)],
        compiler_params=pltpu.CompilerParams(vmem_limit_bytes=VMEM_LIMIT),
        name="decode_attn",
    )(page_ids, slope_rows, q_rep, k_new, v_new, cache_k, cache_v)


def _silu_gate(a, g):
    return g * (1.0 / (1.0 + jnp.exp(-g))) * a


def _prompt_ffn_kernel(x_ref, attn_ref, u_ref, wpool_ref, pscale_ref, wout_ref, fng_ref, wup_ref,
                       cw_ref, cb_ref, wdown_ref, y_ref, cs_ref,
                       ucarry_ref, upcarry_ref, h2_ref, acc_ref):
    t = pl.program_id(1)
    tm = x_ref.shape[0]

    @pl.when(t == 0)
    def _():
        ucarry_ref[...] = jnp.zeros_like(ucarry_ref)
        upcarry_ref[...] = jnp.zeros_like(upcarry_ref)

    u = u_ref[...]
    ext = jnp.concatenate([ucarry_ref[...], u], axis=0)
    ucarry_ref[...] = u[tm - POOL_CARRY:, :]
    row = lax.broadcasted_iota(I32, (tm, 1), 0)
    n_seen = t * tm + row + 1
    outs = []
    for g, w in enumerate(POOL_WINDOWS):
        s = ext[:, g * POOL_GROUP:(g + 1) * POOL_GROUP]
        sh = 1
        while sh < w:
            s = s + pltpu.roll(s, sh, 0)
            sh *= 2
        cnt = jnp.minimum(w, n_seen).astype(F32)
        pooled = s[POOL_CARRY:, :] / cnt - u[:, g * POOL_GROUP:(g + 1) * POOL_GROUP]
        outs.append(jnp.dot(pooled.astype(BF16), wpool_ref[g], preferred_element_type=F32))
    pool = jnp.concatenate(outs, axis=1) * pscale_ref[...]
    mix = jnp.concatenate([attn_ref[...], pool.astype(BF16)], axis=1)
    x1 = x_ref[...] + jnp.dot(mix, wout_ref[...], preferred_element_type=F32)
    h2_ref[...] = _rms(x1, fng_ref[...]).astype(BF16)
    acc_ref[...] = x1

    def ff_body(c, carry):
        h2 = h2_ref[...]
        halves = []
        for ag in range(2):
            up = jnp.dot(h2, wup_ref[ag, c], preferred_element_type=F32)
            prev = upcarry_ref[ag, c]
            p1 = prev[SUBLANES - 1:SUBLANES, :]
            p2 = prev[SUBLANES - 2:SUBLANES - 1, :]
            up1 = jnp.where(row == 0, p1, pltpu.roll(up, 1, 0))
            up2 = jnp.where(row == 0, p2, jnp.where(row == 1, p1, pltpu.roll(up, 2, 0)))
            cw = cw_ref[ag, c]
            conv = cb_ref[ag, c] + cw[0:1, :] * up2 + cw[1:2, :] * up1 + cw[2:3, :] * up
            last = up[tm - SUBLANES:, :]
            upcarry_ref[ag, c] = last
            cs_ref[0, ag, c] = last
            halves.append(conv)
        gated = _silu_gate(halves[0], halves[1]).astype(BF16)
        acc_ref[...] += jnp.dot(gated, wdown_ref[c], preferred_element_type=F32)
        return carry
    lax.fori_loop(0, N_FF_CHUNKS, ff_body, 0)
    y_ref[...] = acc_ref[...]


def _weight_specs(layer, n_grid):
    zeros = (0,) * n_grid

    def spec(shape):
        nd = len(shape)
        return pl.BlockSpec((None,) + shape, lambda *_: (layer,) + (0,) * nd, pipeline_mode=pl.Buffered(1))
    del zeros
    return [
        spec((len(POOL_WINDOWS), POOL_GROUP, POOL_GROUP)),
        spec((1, POOL_WIDTH)),
        spec((D_MODEL, D_MODEL)),
        spec((1, D_MODEL)),
        spec((2, N_FF_CHUNKS, D_MODEL, FF_CHUNK)),
        spec((2, N_FF_CHUNKS, SUBLANES, FF_CHUNK)),
        spec((2, N_FF_CHUNKS, 1, FF_CHUNK)),
        spec((N_FF_CHUNKS, FF_CHUNK, D_MODEL)),
    ]


def _prompt_ffn(x, attn, u, layer, weights, n_batch, tm):
    n = x.shape[0]
    tiles = n // n_batch // tm
    row = lambda b, t: (b * tiles + t, 0)
    return pl.pallas_call(
        _prompt_ffn_kernel,
        grid=(n_batch, tiles),
        in_specs=[pl.BlockSpec((tm, D_MODEL), row), pl.BlockSpec((tm, ATTN_WIDTH), row),
                  pl.BlockSpec((tm, POOL_WIDTH), row)] + _weight_specs(layer, 2),
        out_specs=[pl.BlockSpec((tm, D_MODEL), row),
                   pl.BlockSpec((1, 2, N_FF_CHUNKS, SUBLANES, FF_CHUNK), lambda b, t: (b, 0, 0, 0, 0))],
        out_shape=[jax.ShapeDtypeStruct((n, D_MODEL), F32),
                   jax.ShapeDtypeStruct((n_batch, 2, N_FF_CHUNKS, SUBLANES, FF_CHUNK), F32)],
        scratch_shapes=[pltpu.VMEM((POOL_CARRY, POOL_WIDTH), F32),
                        pltpu.VMEM((2, N_FF_CHUNKS, SUBLANES, FF_CHUNK), F32),
                        pltpu.VMEM((tm, D_MODEL), BF16),
                        pltpu.VMEM((tm, D_MODEL), F32)],
        compiler_params=pltpu.CompilerParams(dimension_semantics=("arbitrary", "arbitrary"),
                                             vmem_limit_bytes=VMEM_LIMIT),
        name="prompt_ffn",
    )(x, attn, u, *weights)


def _sample_ffn_kernel(x_ref, attn_ref, u_ref, pstate_ref, cstate_ref, wpool_ref, pscale_ref, wout_ref,
                       fng_ref, wup_ref, cw_ref, cb_ref, wdown_ref, y_ref, cs_ref, h2_ref, acc_ref,
                       *, n_seq):
    n = x_ref.shape[0]
    u = u_ref[...]
    ext = jnp.concatenate([pstate_ref[...], u], axis=0)
    sums = {1: ext}
    w = 1
    while w < max(POOL_WINDOWS):
        prev = sums[w]
        sums[2 * w] = prev[w * n_seq:, :] + prev[:prev.shape[0] - w * n_seq, :]
        w *= 2
    outs = []
    for g, w in enumerate(POOL_WINDOWS):
        r0 = (POOL_BUF + 1 - w) * n_seq
        win = sums[w][r0:r0 + n, g * POOL_GROUP:(g + 1) * POOL_GROUP]
        pooled = win / float(w) - u[:, g * POOL_GROUP:(g + 1) * POOL_GROUP]
        outs.append(jnp.dot(pooled.astype(BF16), wpool_ref[g], preferred_element_type=F32))
    pool = jnp.concatenate(outs, axis=1) * pscale_ref[...]
    mix = jnp.concatenate([attn_ref[...], pool.astype(BF16)], axis=1)
    x1 = x_ref[...] + jnp.dot(mix, wout_ref[...], preferred_element_type=F32)
    h2_ref[...] = _rms(x1, fng_ref[...]).astype(BF16)
    acc_ref[...] = x1
    n_prev = (CONV_WIDTH - 1) * n_seq

    def ff_body(c, carry):
        h2 = h2_ref[...]
        halves = []
        for ag in range(2):
            up = jnp.dot(h2, wup_ref[ag, c], preferred_element_type=F32)
            ext_up = jnp.concatenate([cstate_ref[ag, c], up], axis=0)
            cw = cw_ref[ag, c]
            conv = cb_ref[ag, c]
            for j in range(CONV_WIDTH):
                conv = conv + cw[j:j + 1, :] * ext_up[j * n_seq:j * n_seq + n, :]
            cs_ref[ag, c] = ext_up[n:, :]
            halves.append(conv)
        gated = _silu_gate(halves[0], halves[1]).astype(BF16)
        acc_ref[...] += jnp.dot(gated, wdown_ref[c], preferred_element_type=F32)
        return carry
    del n_prev
    lax.fori_loop(0, N_FF_CHUNKS, ff_body, 0)
    y_ref[...] = acc_ref[...]


def _sample_ffn(x, attn, u, pool_state, conv_state, layer, weights, n_seq):
    n = x.shape[0]
    n_prev = (CONV_WIDTH - 1) * n_seq
    full = lambda shape: pl.BlockSpec(shape, lambda i: (0,) * len(shape))
    lay = lambda shape: pl.BlockSpec((None,) + shape, lambda i: (layer,) + (0,) * len(shape))
    return pl.pallas_call(
        functools.partial(_sample_ffn_kernel, n_seq=n_seq),
        grid=(1,),
        in_specs=[full((n, D_MODEL)), full((n, ATTN_WIDTH)), full((n, POOL_WIDTH)),
                  lay((POOL_BUF * n_seq, POOL_WIDTH)),
                  lay((2, N_FF_CHUNKS, n_prev, FF_CHUNK))] + _weight_specs(layer, 1),
        out_specs=[full((n, D_MODEL)), full((2, N_FF_CHUNKS, n_prev, FF_CHUNK))],
        out_shape=[jax.ShapeDtypeStruct((n, D_MODEL), F32),
                   jax.ShapeDtypeStruct((2, N_FF_CHUNKS, n_prev, FF_CHUNK), F32)],
        scratch_shapes=[pltpu.VMEM((n, D_MODEL), BF16), pltpu.VMEM((n, D_MODEL), F32)],
        compiler_params=pltpu.CompilerParams(dimension_semantics=("arbitrary",), vmem_limit_bytes=VMEM_LIMIT),
        name="sample_ffn",
    )(x, attn, u, pool_state, conv_state, *weights)


def _ff_cols(a):
    return a.reshape(a.shape[:-1] + (2, N_FF_CHUNKS, FF_CHUNK))


def kernel(x_prompt, x_sample, cache_k, cache_v, state_pool, state_conv, page_table, attn_norm, w_in, q_norm, k_norm, w_pool, pool_scale, w_out, ffn_norm, w_up, conv_w, conv_b, w_down):
    depth = w_in.shape[0]
    bp, t_p, _ = x_prompt.shape
    bs, t_s, _ = x_sample.shape
    n_phys = cache_k.shape[1]
    n_pages = page_table.shape[1]
    past_len = n_pages * PAGE_SIZE
    assert t_p % MOBA_BLOCK == 0 and past_len % MOBA_BLOCK == 0 and t_p >= POOL_CARRY
    assert t_s <= SUBLANES and 2 * (t_p // MOBA_BLOCK) <= HEAD_DIM

    w_in_b = w_in.astype(BF16)
    w_out_b = w_out.astype(BF16)
    w_pool_b = w_pool.astype(BF16)
    w_up_c = jnp.transpose(_ff_cols(w_up), (0, 2, 3, 1, 4)).astype(BF16)
    w_down_c = w_down.astype(BF16).reshape(depth, N_FF_CHUNKS, FF_CHUNK, D_MODEL)
    conv_w_c = jnp.transpose(_ff_cols(conv_w), (0, 2, 3, 1, 4))
    conv_w_c = jnp.pad(conv_w_c, ((0, 0), (0, 0), (0, 0), (0, SUBLANES - CONV_WIDTH), (0, 0)))
    conv_b_c = _ff_cols(conv_b)[:, :, :, None, :]
    ffn_weights = (w_pool_b, pool_scale[:, None, :], w_out_b, ffn_norm[:, None, :],
                   w_up_c, conv_w_c, conv_b_c, w_down_c)
    attn_gain = attn_norm[:, None, :]
    qk_gain = jnp.concatenate([jnp.tile(q_norm, (1, N_HEADS)), jnp.tile(k_norm, (1, N_HEADS))], axis=1)[:, None, :]
    head_of = np.arange(2 * LANES) % LANES // HEAD_DIM
    gsum = jnp.asarray(head_of[:, None] == head_of[None, :LANES], BF16)
    slopes_np = np.exp2(-np.arange(1, N_HEADS + 1, dtype=np.float64)).astype(np.float32)
    slopes = jnp.asarray(slopes_np)
    n_rows = t_s * N_HEADS
    slope_rows = jnp.asarray(np.tile(slopes_np, t_s)[:, None])

    cache_k2 = cache_k.reshape(depth * n_phys, PAGE_SIZE, ATTN_WIDTH)
    cache_v2 = cache_v.reshape(depth * n_phys, PAGE_SIZE, ATTN_WIDTH)
    pool_state_tm = jnp.transpose(state_pool, (0, 2, 1, 3)).reshape(depth, POOL_BUF * bs, POOL_WIDTH)
    conv_state_c = jnp.transpose(_ff_cols(state_conv), (0, 3, 4, 2, 1, 5))
    conv_state_c = conv_state_c.reshape(depth, 2, N_FF_CHUNKS, (CONV_WIDTH - 1) * bs, FF_CHUNK)

    n_p = bp * t_p
    tm_p = 512 if t_p % 512 == 0 else MOBA_BLOCK
    y_p = x_prompt.reshape(n_p, D_MODEL)
    y_s = jnp.transpose(x_sample, (1, 0, 2)).reshape(t_s * bs, D_MODEL)
    pad_new = 2 * SUBLANES - t_s

    outs = [[] for _ in range(8)]
    for l in range(depth):
        q, k, v, u = _inproj(y_p, l, attn_gain, w_in_b, qk_gain, gsum, tm_p)
        attn = _prompt_attn(slopes, q.reshape(bp, t_p, ATTN_WIDTH), k.reshape(bp, t_p, ATTN_WIDTH),
                            v.reshape(bp, t_p, ATTN_WIDTH))
        y_p, cs_p = _prompt_ffn(y_p, attn.reshape(n_p, ATTN_WIDTH), u, l, ffn_weights, bp, tm_p)
        outs[0].append(k.reshape(bp, t_p, N_HEADS, HEAD_DIM))
        outs[1].append(v.reshape(bp, t_p, N_HEADS, HEAD_DIM))
        outs[2].append(u.reshape(bp, t_p, POOL_WIDTH)[:, t_p - POOL_BUF:, :])
        cs_p = cs_p[:, :, :, SUBLANES - (CONV_WIDTH - 1):, :]
        outs[3].append(jnp.transpose(cs_p, (0, 3, 1, 2, 4)).reshape(bp, CONV_WIDTH - 1, 2 * D_FF))

        qs, ks, vs, us = _inproj(y_s, l, attn_gain, w_in_b, qk_gain, gsum, t_s * bs)
        to_seq = lambda a: jnp.transpose(a.reshape(t_s, bs, ATTN_WIDTH), (1, 0, 2))
        qs_b, ks_b, vs_b, us_b = to_seq(qs), to_seq(ks), to_seq(vs), to_seq(us)
        q_rep = jnp.repeat(qs_b, N_HEADS, axis=1)
        k_new = jnp.pad(ks_b, ((0, 0), (0, pad_new), (0, 0)))
        v_new = jnp.pad(vs_b, ((0, 0), (0, pad_new), (0, 0)))
        attn_s = _decode_attn(page_table + l * n_phys, slope_rows, q_rep, k_new, v_new,
                              cache_k2, cache_v2, past_len, t_s)
        attn_s = jnp.transpose(attn_s, (1, 0, 2)).reshape(t_s * bs, ATTN_WIDTH).astype(BF16)
        y_s, cs_s = _sample_ffn(y_s, attn_s, us, pool_state_tm, conv_state_c, l, ffn_weights, bs)
        outs[4].append(ks_b.reshape(bs, t_s, N_HEADS, HEAD_DIM))
        outs[5].append(vs_b.reshape(bs, t_s, N_HEADS, HEAD_DIM))
        outs[6].append(jnp.concatenate([state_pool[l], us_b], axis=1)[:, t_s:, :])
        cs_s = cs_s.reshape(2, N_FF_CHUNKS, CONV_WIDTH - 1, bs, FF_CHUNK)
        outs[7].append(jnp.transpose(cs_s, (3, 2, 0, 1, 4)).reshape(bs, CONV_WIDTH - 1, 2 * D_FF))

    y_prompt = y_p.reshape(bp, t_p, D_MODEL)
    y_sample = jnp.transpose(y_s.reshape(t_s, bs, D_MODEL), (1, 0, 2))
    return (y_prompt, y_sample) + tuple(jnp.stack(o) for o in outs)
```

```python
import functools

import numpy as np
import jax
import jax.numpy as jnp
from jax import lax
from jax.experimental import pallas as pl
from jax.experimental.pallas import tpu as pltpu

F32 = jnp.float32
BF16 = jnp.bfloat16
I32 = jnp.int32

D_MODEL = 1024
HEAD_DIM = 64
N_HEADS = 8
ATTN_WIDTH = N_HEADS * HEAD_DIM
POOL_WIDTH = 512
MOBA_BLOCK = 256
MOBA_TOPK = 3
POOL_WINDOWS = (2, 4, 8, 16)
POOL_GROUP = 128
POOL_BUF = 15
D_FF = 2816
CONV_WIDTH = 3
RMS_EPS = 1e-6
PAGE_SIZE = 128
PAGES_PER_BLOCK = MOBA_BLOCK // PAGE_SIZE

LANES = 128
SUBLANES = 8
FF_CHUNK = 256
N_FF_CHUNKS = D_FF // FF_CHUNK
POOL_CARRY = 16
NEG_BIG = -1e30
QK_SCALE = HEAD_DIM ** -0.5
DEC_GROUP = 4
DEC_RING = 3
VMEM_LIMIT = 56 * 1024 * 1024

_NT = (((1,), (1,)), ((), ()))


def _rms(x, gain):
    ms = jnp.mean(x * x, axis=-1, keepdims=True)
    return x * lax.rsqrt(ms + RMS_EPS) * gain


def _head_rms_rows(blk, gsum, gain):
    sq = blk * blk
    hi = sq.astype(BF16)
    lo = (sq - hi.astype(F32)).astype(BF16)
    ssum = jnp.dot(jnp.concatenate([hi, lo], axis=1), gsum, preferred_element_type=F32)
    return blk * lax.rsqrt(ssum * (1.0 / HEAD_DIM) + RMS_EPS) * gain


def _inproj_prompt_kernel(x_ref, g_ref, wqu_ref, wkvt_ref, qg_ref, kgt_ref, gsum_ref, *rest):
    q_ref, u_ref, kt_ref, vt_ref = rest[-4:]
    tm = x_ref.shape[0]
    h = _rms(x_ref[...], g_ref[...]).astype(BF16)
    qu = jnp.dot(h, wqu_ref[...], preferred_element_type=F32)
    kvt = lax.dot_general(wkvt_ref[...], h, _NT, preferred_element_type=F32)
    gsum = gsum_ref[...]
    for j in range(ATTN_WIDTH // LANES):
        cols = slice(j * LANES, (j + 1) * LANES)
        q_ref[:, cols] = _head_rms_rows(qu[:, cols], gsum, qg_ref[:, cols])
    u_ref[...] = qu[:, ATTN_WIDTH:]
    gain_t = jnp.concatenate([kgt_ref[...]] * (tm // LANES), axis=1)
    for hd in range(N_HEADS):
        rows = slice(hd * HEAD_DIM, (hd + 1) * HEAD_DIM)
        blk = kvt[rows, :]
        ms = jnp.sum(blk * blk, axis=0, keepdims=True) * (1.0 / HEAD_DIM)
        kt_ref[rows, :] = blk * lax.rsqrt(ms + RMS_EPS) * gain_t
    vt_ref[...] = kvt[ATTN_WIDTH:, :]


def _inproj_rows_kernel(x_ref, g_ref, wqu_ref, wkvt_ref, qg_ref, kg_ref, gsum_ref, q_ref, k_ref, v_ref, u_ref):
    h = _rms(x_ref[...], g_ref[...]).astype(BF16)
    qu = jnp.dot(h, wqu_ref[...], preferred_element_type=F32)
    kv = lax.dot_general(h, wkvt_ref[...], _NT, preferred_element_type=F32)
    gsum = gsum_ref[...]
    for j in range(ATTN_WIDTH // LANES):
        cols = slice(j * LANES, (j + 1) * LANES)
        q_ref[:, cols] = _head_rms_rows(qu[:, cols], gsum, qg_ref[:, cols])
        k_ref[:, cols] = _head_rms_rows(kv[:, cols], gsum, kg_ref[:, cols])
    v_ref[...] = kv[:, ATTN_WIDTH:]
    u_ref[...] = qu[:, ATTN_WIDTH:]


def _inproj_weight_specs(layer):
    lay = lambda shape, **kw: pl.BlockSpec((None,) + shape, lambda i: (layer,) + (0,) * len(shape), **kw)
    return [lay((1, D_MODEL)),
            lay((D_MODEL, ATTN_WIDTH + POOL_WIDTH), pipeline_mode=pl.Buffered(1)),
            lay((2 * ATTN_WIDTH, D_MODEL), pipeline_mode=pl.Buffered(1)),
            lay((1, ATTN_WIDTH))]


def _inproj_prompt(x, layer, n_batch, weights, k_gain_t, gsum, tm, kv_all):
    n = x.shape[0]
    kt_all, vt_all = kv_all
    tiles = n // n_batch // tm
    row = lambda i: (i, 0)
    kv_spec = pl.BlockSpec((None, None, ATTN_WIDTH, tm), lambda i: (layer, i // tiles, 0, i % tiles))
    in_specs = [pl.BlockSpec((tm, D_MODEL), row)] + _inproj_weight_specs(layer) + [
        pl.BlockSpec((None, HEAD_DIM, LANES), lambda i: (layer, 0, 0)),
        pl.BlockSpec((2 * LANES, LANES), lambda i: (0, 0))]
    args = [x, *weights, k_gain_t, gsum]
    aliases = {len(args): 2, len(args) + 1: 3}
    in_specs += [pl.BlockSpec(memory_space=pl.ANY)] * 2
    args += [kt_all, vt_all]
    rows_out = jax.ShapeDtypeStruct((n, ATTN_WIDTH), F32)
    kv_out = jax.ShapeDtypeStruct(kt_all.shape, F32)
    return pl.pallas_call(
        _inproj_prompt_kernel,
        grid=(n // tm,),
        in_specs=in_specs,
        out_specs=[pl.BlockSpec((tm, ATTN_WIDTH), row)] * 2 + [kv_spec] * 2,
        out_shape=[rows_out, rows_out, kv_out, kv_out],
        input_output_aliases=aliases,
        compiler_params=pltpu.CompilerParams(dimension_semantics=("arbitrary",), vmem_limit_bytes=VMEM_LIMIT),
        name="inproj_prompt",
    )(*args)


def _inproj_rows(x, layer, weights, k_gain, gsum):
    n = x.shape[0]
    full = lambda shape: pl.BlockSpec(shape, lambda i: (0,) * len(shape))
    out = jax.ShapeDtypeStruct((n, ATTN_WIDTH), F32)
    return pl.pallas_call(
        _inproj_rows_kernel,
        grid=(1,),
        in_specs=[full((n, D_MODEL))] + _inproj_weight_specs(layer) + [
            pl.BlockSpec((None, 1, ATTN_WIDTH), lambda i: (layer, 0, 0)), full((2 * LANES, LANES))],
        out_specs=[full((n, ATTN_WIDTH))] * 4,
        out_shape=[out] * 4,
        compiler_params=pltpu.CompilerParams(dimension_semantics=("arbitrary",), vmem_limit_bytes=VMEM_LIMIT),
        name="inproj_rows",
    )(x, *weights, k_gain, gsum)


def _prompt_attn_kernel(slopes_ref, q_ref, kt_ref, vt_ref, o_ref, qa_ref, ka_ref, vb_ref, s_ref):
    t_len = q_ref.shape[1]
    nb = t_len // MOBA_BLOCK
    pair = pl.program_id(1)
    q = q_ref[0]
    kt = kt_ref[...]
    lane = lax.broadcasted_iota(I32, (1, LANES), 1)
    in_h0 = lane < HEAD_DIM

    blk_of_key = lax.broadcasted_iota(I32, (nb, t_len), 1) // MOBA_BLOCK
    n_idx = lax.broadcasted_iota(I32, (nb, t_len), 0)
    mean_w = jnp.where(blk_of_key == n_idx, 1.0 / MOBA_BLOCK, 0.0)
    kmeans = lax.dot_general(mean_w, kt, _NT, precision=lax.Precision.HIGHEST,
                             preferred_element_type=F32)
    km2 = jnp.concatenate([jnp.where(in_h0, kmeans, 0.0), jnp.where(in_h0, 0.0, kmeans)], axis=0)
    gate = lax.dot_general(km2, q, _NT, precision=lax.Precision.HIGHEST,
                           preferred_element_type=F32)

    q_blk = blk_of_key
    aux_rows = []
    for hh in range(2):
        g = gate[hh * nb:(hh + 1) * nb]
        slope = slopes_ref[2 * pair + hh]
        rank = jnp.zeros((nb, t_len), F32)
        for m in range(nb):
            gm = g[m:m + 1, :]
            beats = jnp.where(gm > g, 1.0, jnp.where(gm == g, jnp.where(m < n_idx, 1.0, 0.0), 0.0))
            rank = rank + jnp.where(m < q_blk, beats, 0.0)
        thresh = jnp.where(n_idx < q_blk, float(MOBA_TOPK), jnp.where(n_idx == q_blk, float(nb + 1), -1.0))
        bias = jnp.where(rank < thresh, slope * (MOBA_BLOCK * n_idx).astype(F32), NEG_BIG)
        slope_rows = jnp.where(n_idx == 0, slope, 0.0)
        aux_rows.append(jnp.concatenate([bias, slope_rows], axis=0))
    pad = jnp.zeros((HEAD_DIM - 2 * nb, t_len), F32)
    aux_t = jnp.concatenate([aux_rows[1], pad, aux_rows[0], pad], axis=0).T
    qs = q * QK_SCALE
    qa_ref[0] = jnp.where(in_h0, qs, aux_t).astype(BF16)
    qa_ref[1] = jnp.where(in_h0, aux_t, qs).astype(BF16)

    a_idx = lax.broadcasted_iota(I32, (HEAD_DIM, t_len), 0)
    pos = lax.broadcasted_iota(I32, (HEAD_DIM, t_len), 1)
    k_aux = jnp.where(a_idx < nb, jnp.where(pos // MOBA_BLOCK == a_idx, 1.0, 0.0),
                      jnp.where(a_idx == nb, (pos % MOBA_BLOCK).astype(F32), 0.0))
    ka_ref[0] = jnp.concatenate([kt[:HEAD_DIM], k_aux], axis=0).astype(BF16)
    ka_ref[1] = jnp.concatenate([k_aux, kt[HEAD_DIM:]], axis=0).astype(BF16)
    vb_ref[...] = vt_ref[...].astype(BF16)

    r_i = lax.broadcasted_iota(I32, (MOBA_BLOCK, MOBA_BLOCK), 0)
    c_i = lax.broadcasted_iota(I32, (MOBA_BLOCK, MOBA_BLOCK), 1)
    causal = c_i <= r_i
    for i in range(nb):
        r0 = i * MOBA_BLOCK
        outs = []
        for hh in range(2):
            qa = qa_ref[hh, r0:r0 + MOBA_BLOCK, :]
            m_part = jnp.full((MOBA_BLOCK, LANES), NEG_BIG, F32)
            for j in range(i + 1):
                keys = slice(j * MOBA_BLOCK, (j + 1) * MOBA_BLOCK)
                s = jnp.dot(qa, ka_ref[hh, :, keys], preferred_element_type=F32)
                if j == i:
                    s = jnp.where(causal, s, NEG_BIG)
                s_ref[hh, :, keys] = s
                m_part = jnp.maximum(m_part, jnp.maximum(s[:, :LANES], s[:, LANES:]))
            m = jnp.broadcast_to(jnp.max(m_part, axis=1, keepdims=True), (MOBA_BLOCK, LANES))
            l_part = jnp.zeros((MOBA_BLOCK, LANES), F32)
            acc = jnp.zeros((MOBA_BLOCK, LANES), F32)
            for j in range(i + 1):
                keys = slice(j * MOBA_BLOCK, (j + 1) * MOBA_BLOCK)
                s = s_ref[hh, :, keys]
                p0 = jnp.exp(s[:, :LANES] - m)
                p1 = jnp.exp(s[:, LANES:] - m)
                l_part = l_part + (p0 + p1)
                p = jnp.concatenate([p0, p1], axis=1).astype(BF16)
                acc = acc + lax.dot_general(p, vb_ref[:, keys], _NT, preferred_element_type=F32)
            outs.append(acc / jnp.sum(l_part, axis=1, keepdims=True))
        o_ref[0, r0:r0 + MOBA_BLOCK, :] = jnp.where(in_h0, outs[0], outs[1]).astype(o_ref.dtype)


def _prompt_attn(slopes, q, kt_all, vt_all, layer):
    b, t_len, _ = q.shape
    q_blk = pl.BlockSpec((1, t_len, LANES), lambda i, j: (i, 0, j))
    kv_blk = pl.BlockSpec((None, None, LANES, t_len), lambda i, j: (layer, i, j, 0))
    return pl.pallas_call(
        _prompt_attn_kernel,
        grid=(b, ATTN_WIDTH // LANES),
        in_specs=[pl.BlockSpec(memory_space=pltpu.SMEM), q_blk, kv_blk, kv_blk],
        out_specs=q_blk,
        out_shape=jax.ShapeDtypeStruct((b, t_len, ATTN_WIDTH), BF16),
        scratch_shapes=[pltpu.VMEM((2, t_len, LANES), BF16), pltpu.VMEM((2, LANES, t_len), BF16),
                        pltpu.VMEM((LANES, t_len), BF16), pltpu.VMEM((2, MOBA_BLOCK, t_len), F32)],
        compiler_params=pltpu.CompilerParams(dimension_semantics=("arbitrary", "arbitrary"),
                                             vmem_limit_bytes=VMEM_LIMIT),
        name="prompt_attn",
    )(slopes, q, kt_all, vt_all)


def _decode_attn_kernel(pt_ref, slope_ref, qrep_ref, knew_ref, vnew_ref, ck_ref, cv_ref, o_ref,
                        buf_ref, sem, s_ref, g_ref, *, past_len, n_new):
    n_seq = qrep_ref.shape[0]
    n_rows = qrep_ref.shape[1]
    n_blk = past_len // MOBA_BLOCK
    grp = buf_ref.shape[1] // PAGES_PER_BLOCK
    grp_pages = buf_ref.shape[1]
    n_grp = n_blk // grp
    per_seq = 2 * n_grp
    n_chunks = n_seq * per_seq
    n_pad = knew_ref.shape[1]

    def copies(src_ref, pages, slot):
        return [pltpu.make_async_copy(src_ref.at[page], buf_ref.at[slot, h], sem.at[slot])
                for h, page in enumerate(pages)]

    def start_chunk(g):
        seq = g // per_seq
        c = g % per_seq
        first_page = (c % n_grp) * grp_pages
        slot = g % DEC_RING
        pages = [pt_ref[seq, first_page + h] for h in range(grp_pages)]

        @pl.when(c < n_grp)
        def _():
            for cp in copies(ck_ref, pages, slot):
                cp.start()

        @pl.when(c >= n_grp)
        def _():
            for cp in copies(cv_ref, pages, slot):
                cp.start()

    def advance(g):
        slot = g % DEC_RING
        for cp in copies(ck_ref, [0] * grp_pages, slot):
            cp.wait()
        nxt = g + DEC_RING - 1

        @pl.when(nxt < n_chunks)
        def _():
            start_chunk(nxt)
        return slot

    for g0 in range(DEC_RING - 1):
        start_chunk(jnp.int32(g0))

    row = lax.broadcasted_iota(I32, (n_rows, 1), 0)
    h_row = row % N_HEADS
    t_row = row // N_HEADS
    head_mask = (lax.broadcasted_iota(I32, (1, ATTN_WIDTH), 1) // HEAD_DIM) == h_row
    slope = slope_ref[...]
    key_lane = lax.broadcasted_iota(I32, (1, MOBA_BLOCK), 1)
    alibi0 = -(slope * (past_len + t_row - key_lane).astype(F32))
    new_idx = lax.broadcasted_iota(I32, (1, n_pad), 1)
    own_ok = jnp.logical_and(new_idx <= t_row, new_idx < n_new)
    own_alibi = -(slope * (t_row - new_idx).astype(F32))

    def seq_body(seq, carry):
        qs = jnp.where(head_mask, qrep_ref[seq], 0.0) * QK_SCALE
        q_hi = qs.astype(BF16)
        q_lo = (qs - q_hi.astype(F32)).astype(BF16)
        q2 = jnp.concatenate([q_hi, q_lo], axis=0)

        def k_body(i, c):
            slot = advance(seq * per_seq + i)
            for b in range(grp):
                n = i * grp + b
                g = jnp.zeros((n_rows, PAGE_SIZE), F32)
                for half in range(PAGES_PER_BLOCK):
                    kf = buf_ref[slot, b * PAGES_PER_BLOCK + half]
                    k_hi = kf.astype(BF16)
                    k_lo = (kf - k_hi.astype(F32)).astype(BF16)
                    s2 = jnp.dot(q2, k_hi, preferred_element_type=F32)
                    s_hl = jnp.dot(q_hi, k_lo, preferred_element_type=F32)
                    s_ref[n, :, half * PAGE_SIZE:(half + 1) * PAGE_SIZE] = s2[:n_rows]
                    g = g + (s2[:n_rows] + s2[n_rows:] + s_hl)
                g_ref[n] = g
            return c
        lax.fori_loop(0, n_grp, k_body, 0)

        cur = [jnp.sum(g_ref[n], axis=1, keepdims=True) for n in range(n_blk)]
        sel = [jnp.zeros((n_rows, 1), F32)] * n_blk
        for _ in range(min(MOBA_TOPK, n_blk)):
            mx = functools.reduce(jnp.maximum, cur)
            first = functools.reduce(jnp.minimum, [jnp.where(c == mx, float(n), float(n_blk))
                                                   for n, c in enumerate(cur)])
            sel = [jnp.where(first == float(n), 1.0, s) for n, s in enumerate(sel)]
            cur = [jnp.where(first == float(n), -jnp.inf, c) for n, c in enumerate(cur)]

        m_run = jnp.full((n_rows, MOBA_BLOCK), NEG_BIG, F32)
        for n in range(n_blk):
            sb = jnp.where(sel[n] > 0.0, s_ref[n] + (alibi0 + slope * float(MOBA_BLOCK * n)), NEG_BIG)
            s_ref[n] = sb
            m_run = jnp.maximum(m_run, sb)

        s_own = lax.dot_general(q_hi, knew_ref[seq].astype(BF16), _NT, preferred_element_type=F32)
        s_own = jnp.where(own_ok, s_own + own_alibi, NEG_BIG)
        m = jnp.maximum(jnp.max(m_run, axis=1, keepdims=True), jnp.max(s_own, axis=1, keepdims=True))
        p_own = jnp.exp(s_own - m)
        l0 = jnp.sum(p_own, axis=1, keepdims=True)
        acc0 = jnp.dot(p_own.astype(BF16), vnew_ref[seq].astype(BF16), preferred_element_type=F32)

        def v_body(i, c):
            acc, l = c
            slot = advance(seq * per_seq + n_grp + i)
            for b in range(grp):
                p = jnp.exp(s_ref[i * grp + b] - m)
                l = l + jnp.sum(p, axis=1, keepdims=True)
                pb = p.astype(BF16)
                for half in range(PAGES_PER_BLOCK):
                    acc = acc + lax.dot_general(pb[:, half * PAGE_SIZE:(half + 1) * PAGE_SIZE],
                                                buf_ref[slot, b * PAGES_PER_BLOCK + half].astype(BF16), _NT,
                                                preferred_element_type=F32)
            return acc, l
        acc, l = lax.fori_loop(0, n_grp, v_body, (acc0, l0))

        o = jnp.where(head_mask, acc / l, 0.0)
        o_ref[seq] = jnp.sum(o.reshape(n_rows // N_HEADS, N_HEADS, ATTN_WIDTH), axis=1)
        return carry
    lax.fori_loop(0, n_seq, seq_body, 0)


def _decode_attn(page_ids, slope_rows, q_rep, k_new, v_new, cache_kt, cache_vt, past_len, n_new):
    n_seq, n_rows, _ = q_rep.shape
    n_blk = past_len // MOBA_BLOCK
    grp = DEC_GROUP if n_blk % DEC_GROUP == 0 else 1
    vmem = pl.BlockSpec(memory_space=pltpu.VMEM)
    return pl.pallas_call(
        functools.partial(_decode_attn_kernel, past_len=past_len, n_new=n_new),
        in_specs=[pl.BlockSpec(memory_space=pltpu.SMEM), vmem, vmem, vmem, vmem,
                  pl.BlockSpec(memory_space=pl.ANY), pl.BlockSpec(memory_space=pl.ANY)],
        out_specs=vmem,
        out_shape=jax.ShapeDtypeStruct((n_seq, n_rows // N_HEADS, ATTN_WIDTH), F32),
        scratch_shapes=[pltpu.VMEM((DEC_RING, grp * PAGES_PER_BLOCK, ATTN_WIDTH, PAGE_SIZE), F32),
                        pltpu.SemaphoreType.DMA((DEC_RING,)),
                        pltpu.VMEM((n_blk, n_rows, MOBA_BLOCK), F32),
                        pltpu.VMEM((n_blk, n_rows, PAGE_SIZE), F32)],
        compiler_params=pltpu.CompilerParams(vmem_limit_bytes=VMEM_LIMIT),
        name="decode_attn",
    )(page_ids, slope_rows, q_rep, k_new, v_new, cache_kt, cache_vt)


def _silu_gate(a, g):
    return g * (1.0 / (1.0 + jnp.exp(-g))) * a


def _prompt_ffn_kernel(x_ref, attn_ref, u_ref, wpool_ref, pscale_ref, wout_ref, fng_ref, wup_ref,
                       cw_ref, cb_ref, wdown_ref, y_ref, cs_ref,
                       ucarry_ref, upcarry_ref, h2_ref, x1_ref, up_ref, gated_ref):
    t = pl.program_id(1)
    tm = x_ref.shape[0]

    @pl.when(t == 0)
    def _():
        ucarry_ref[...] = jnp.zeros_like(ucarry_ref)
        upcarry_ref[...] = jnp.zeros_like(upcarry_ref)

    u = u_ref[...]
    ext = jnp.concatenate([ucarry_ref[...], u], axis=0)
    ucarry_ref[...] = u[tm - POOL_CARRY:, :]
    row = lax.broadcasted_iota(I32, (tm, 1), 0)
    n_seen = t * tm + row + 1
    outs = []
    for g, w in enumerate(POOL_WINDOWS):
        s = ext[:, g * POOL_GROUP:(g + 1) * POOL_GROUP]
        sh = 1
        while sh < w:
            s = s + pltpu.roll(s, sh, 0)
            sh *= 2
        cnt = jnp.minimum(w, n_seen).astype(F32)
        pooled = s[POOL_CARRY:, :] / cnt - u[:, g * POOL_GROUP:(g + 1) * POOL_GROUP]
        outs.append(jnp.dot(pooled.astype(BF16), wpool_ref[g], preferred_element_type=F32))
    pool = jnp.concatenate(outs, axis=1) * pscale_ref[...]
    mix = jnp.concatenate([attn_ref[...], pool.astype(BF16)], axis=1)
    x1 = x_ref[...] + jnp.dot(mix, wout_ref[...], preferred_element_type=F32)
    h2_ref[...] = _rms(x1, fng_ref[...]).astype(BF16)
    x1_ref[...] = x1
    row8 = lax.broadcasted_iota(I32, (SUBLANES, 1), 0)

    def up_proj(c, slot):
        h2 = h2_ref[...]
        for ag in range(2):
            up_ref[slot, ag] = jnp.dot(h2, wup_ref[ag, c], preferred_element_type=F32)

    def conv_gate(c, slot):
        halves = []
        for ag in range(2):
            up = up_ref[slot, ag]
            prev = upcarry_ref[ag, c]
            p1 = prev[SUBLANES - 1:SUBLANES, :]
            p2 = prev[SUBLANES - 2:SUBLANES - 1, :]
            r1 = pltpu.roll(up, 1, 0)
            r2 = pltpu.roll(up, 2, 0)
            top1 = jnp.where(row8 == 0, p1, r1[:SUBLANES])
            top2 = jnp.where(row8 == 0, p2, jnp.where(row8 == 1, p1, r2[:SUBLANES]))
            up1 = jnp.concatenate([top1, r1[SUBLANES:]], axis=0)
            up2 = jnp.concatenate([top2, r2[SUBLANES:]], axis=0)
            cw = cw_ref[ag, c]
            conv = cb_ref[ag, c] + cw[0:1, :] * up2 + cw[1:2, :] * up1 + cw[2:3, :] * up
            last = up[tm - SUBLANES:, :]
            upcarry_ref[ag, c] = last
            cs_ref[0, ag, c] = last
            halves.append(conv)
        gated_ref[c] = _silu_gate(halves[0], halves[1]).astype(BF16)

    up_proj(0, 0)
    for c in range(N_FF_CHUNKS - 1):
        up_proj(c + 1, (c + 1) % 2)
        conv_gate(c, c % 2)
    conv_gate(N_FF_CHUNKS - 1, (N_FF_CHUNKS - 1) % 2)
    gated = jnp.concatenate([gated_ref[c] for c in range(N_FF_CHUNKS)], axis=1)
    y_ref[...] = x1_ref[...] + jnp.dot(gated, wdown_ref[...], preferred_element_type=F32)


def _weight_specs(layer):
    def spec(shape):
        nd = len(shape)
        return pl.BlockSpec((None,) + shape, lambda *_: (layer,) + (0,) * nd, pipeline_mode=pl.Buffered(1))
    return [
        spec((len(POOL_WINDOWS), POOL_GROUP, POOL_GROUP)),
        spec((1, POOL_WIDTH)),
        spec((D_MODEL, D_MODEL)),
        spec((1, D_MODEL)),
        spec((2, N_FF_CHUNKS, D_MODEL, FF_CHUNK)),
        spec((2, N_FF_CHUNKS, SUBLANES, FF_CHUNK)),
        spec((2, N_FF_CHUNKS, 1, FF_CHUNK)),
        spec((D_FF, D_MODEL)),
    ]


def _prompt_ffn(x, attn, u, layer, weights, n_batch, tm):
    n = x.shape[0]
    tiles = n // n_batch // tm
    row = lambda b, t: (b * tiles + t, 0)
    return pl.pallas_call(
        _prompt_ffn_kernel,
        grid=(n_batch, tiles),
        in_specs=[pl.BlockSpec((tm, D_MODEL), row), pl.BlockSpec((tm, ATTN_WIDTH), row),
                  pl.BlockSpec((tm, POOL_WIDTH), row)] + _weight_specs(layer),
        out_specs=[pl.BlockSpec((tm, D_MODEL), row),
                   pl.BlockSpec((1, 2, N_FF_CHUNKS, SUBLANES, FF_CHUNK), lambda b, t: (b, 0, 0, 0, 0))],
        out_shape=[jax.ShapeDtypeStruct((n, D_MODEL), F32),
                   jax.ShapeDtypeStruct((n_batch, 2, N_FF_CHUNKS, SUBLANES, FF_CHUNK), F32)],
        scratch_shapes=[pltpu.VMEM((POOL_CARRY, POOL_WIDTH), F32),
                        pltpu.VMEM((2, N_FF_CHUNKS, SUBLANES, FF_CHUNK), F32),
                        pltpu.VMEM((tm, D_MODEL), BF16),
                        pltpu.VMEM((tm, D_MODEL), F32),
                        pltpu.VMEM((2, 2, tm, FF_CHUNK), F32),
                        pltpu.VMEM((N_FF_CHUNKS, tm, FF_CHUNK), BF16)],
        compiler_params=pltpu.CompilerParams(dimension_semantics=("arbitrary", "arbitrary"),
                                             vmem_limit_bytes=VMEM_LIMIT),
        name="prompt_ffn",
    )(x, attn, u, *weights)


def _sample_ffn_kernel(x_ref, attn_ref, u_ref, pstate_ref, cstate_ref, wpool_ref, pscale_ref, wout_ref,
                       fng_ref, wup_ref, cw_ref, cb_ref, wdown_ref, y_ref, cs_ref, h2_ref, gated_ref,
                       *, n_seq):
    n = x_ref.shape[0]
    u = u_ref[...]
    ext = jnp.concatenate([pstate_ref[...], u], axis=0)
    sums = {1: ext}
    w = 1
    while w < max(POOL_WINDOWS):
        prev = sums[w]
        sums[2 * w] = prev[w * n_seq:, :] + prev[:prev.shape[0] - w * n_seq, :]
        w *= 2
    outs = []
    for g, w in enumerate(POOL_WINDOWS):
        r0 = (POOL_BUF + 1 - w) * n_seq
        win = sums[w][r0:r0 + n, g * POOL_GROUP:(g + 1) * POOL_GROUP]
        pooled = win / float(w) - u[:, g * POOL_GROUP:(g + 1) * POOL_GROUP]
        outs.append(jnp.dot(pooled.astype(BF16), wpool_ref[g], preferred_element_type=F32))
    pool = jnp.concatenate(outs, axis=1) * pscale_ref[...]
    mix = jnp.concatenate([attn_ref[...], pool.astype(BF16)], axis=1)
    x1 = x_ref[...] + jnp.dot(mix, wout_ref[...], preferred_element_type=F32)
    h2_ref[...] = _rms(x1, fng_ref[...]).astype(BF16)

    def ff_body(c, carry):
        h2 = h2_ref[...]
        halves = []
        for ag in range(2):
            up = jnp.dot(h2, wup_ref[ag, c], preferred_element_type=F32)
            ext_up = jnp.concatenate([cstate_ref[ag, c], up], axis=0)
            cw = cw_ref[ag, c]
            conv = cb_ref[ag, c]
            for j in range(CONV_WIDTH):
                conv = conv + cw[j:j + 1, :] * ext_up[j * n_seq:j * n_seq + n, :]
            cs_ref[ag, c] = ext_up[n:, :]
            halves.append(conv)
        gated_ref[c] = _silu_gate(halves[0], halves[1]).astype(BF16)
        return carry
    lax.fori_loop(0, N_FF_CHUNKS, ff_body, 0)
    gated = jnp.concatenate([gated_ref[c] for c in range(N_FF_CHUNKS)], axis=1)
    y_ref[...] = x1 + jnp.dot(gated, wdown_ref[...], preferred_element_type=F32)


def _sample_ffn(x, attn, u, pool_state, conv_state, layer, weights, n_seq):
    n = x.shape[0]
    n_prev = (CONV_WIDTH - 1) * n_seq
    full = lambda shape: pl.BlockSpec(shape, lambda i: (0,) * len(shape))
    lay = lambda shape: pl.BlockSpec((None,) + shape, lambda i: (layer,) + (0,) * len(shape))
    return pl.pallas_call(
        functools.partial(_sample_ffn_kernel, n_seq=n_seq),
        grid=(1,),
        in_specs=[full((n, D_MODEL)), full((n, ATTN_WIDTH)), full((n, POOL_WIDTH)),
                  lay((POOL_BUF * n_seq, POOL_WIDTH)),
                  lay((2, N_FF_CHUNKS, n_prev, FF_CHUNK))] + _weight_specs(layer),
        out_specs=[full((n, D_MODEL)), full((2, N_FF_CHUNKS, n_prev, FF_CHUNK))],
        out_shape=[jax.ShapeDtypeStruct((n, D_MODEL), F32),
                   jax.ShapeDtypeStruct((2, N_FF_CHUNKS, n_prev, FF_CHUNK), F32)],
        scratch_shapes=[pltpu.VMEM((n, D_MODEL), BF16), pltpu.VMEM((N_FF_CHUNKS, n, FF_CHUNK), BF16)],
        compiler_params=pltpu.CompilerParams(dimension_semantics=("arbitrary",), vmem_limit_bytes=VMEM_LIMIT),
        name="sample_ffn",
    )(x, attn, u, pool_state, conv_state, *weights)


def _ff_cols(a):
    return a.reshape(a.shape[:-1] + (2, N_FF_CHUNKS, FF_CHUNK))


def kernel(x_prompt, x_sample, cache_k, cache_v, state_pool, state_conv, page_table, attn_norm, w_in, q_norm, k_norm, w_pool, pool_scale, w_out, ffn_norm, w_up, conv_w, conv_b, w_down):
    depth = w_in.shape[0]
    bp, t_p, _ = x_prompt.shape
    bs, t_s, _ = x_sample.shape
    n_phys = cache_k.shape[1]
    n_pages = page_table.shape[1]
    past_len = n_pages * PAGE_SIZE
    assert t_p % MOBA_BLOCK == 0 and past_len % MOBA_BLOCK == 0 and t_p >= POOL_CARRY
    assert t_s <= SUBLANES and 2 * (t_p // MOBA_BLOCK) <= HEAD_DIM

    w_qu = jnp.concatenate([w_in[:, :, :ATTN_WIDTH], w_in[:, :, 3 * ATTN_WIDTH:]], axis=2).astype(BF16)
    w_kvt = jnp.transpose(w_in[:, :, ATTN_WIDTH:3 * ATTN_WIDTH], (0, 2, 1)).astype(BF16)
    q_gain = jnp.tile(q_norm, (1, N_HEADS))[:, None, :]
    k_gain = jnp.tile(k_norm, (1, N_HEADS))[:, None, :]
    k_gain_t = jnp.broadcast_to(k_norm[:, :, None], (depth, HEAD_DIM, LANES))
    in_weights = (attn_norm[:, None, :], w_qu, w_kvt, q_gain)
    w_up_c = jnp.transpose(_ff_cols(w_up), (0, 2, 3, 1, 4)).astype(BF16)
    conv_w_c = jnp.transpose(_ff_cols(conv_w), (0, 2, 3, 1, 4))
    conv_w_c = jnp.pad(conv_w_c, ((0, 0), (0, 0), (0, 0), (0, SUBLANES - CONV_WIDTH), (0, 0)))
    conv_b_c = _ff_cols(conv_b)[:, :, :, None, :]
    ffn_weights = (w_pool.astype(BF16), pool_scale[:, None, :], w_out.astype(BF16), ffn_norm[:, None, :],
                   w_up_c, conv_w_c, conv_b_c, w_down.astype(BF16))
    head_of = np.arange(2 * LANES) % LANES // HEAD_DIM
    gsum = jnp.asarray(head_of[:, None] == head_of[None, :LANES], BF16)
    slopes_np = np.exp2(-np.arange(1, N_HEADS + 1, dtype=np.float64)).astype(np.float32)
    slopes = jnp.asarray(slopes_np)
    slope_rows = jnp.asarray(np.tile(slopes_np, t_s)[:, None])

    cache_kt = jnp.transpose(cache_k, (0, 1, 3, 4, 2)).reshape(depth * n_phys, ATTN_WIDTH, PAGE_SIZE)
    cache_vt = jnp.transpose(cache_v, (0, 1, 3, 4, 2)).reshape(depth * n_phys, ATTN_WIDTH, PAGE_SIZE)
    pool_state_tm = jnp.transpose(state_pool, (0, 2, 1, 3)).reshape(depth, POOL_BUF * bs, POOL_WIDTH)
    conv_state_c = jnp.transpose(_ff_cols(state_conv), (0, 3, 4, 2, 1, 5))
    conv_state_c = conv_state_c.reshape(depth, 2, N_FF_CHUNKS, (CONV_WIDTH - 1) * bs, FF_CHUNK)

    n_p = bp * t_p
    tm_p = 512 if t_p % 512 == 0 else MOBA_BLOCK
    y_p = x_prompt.reshape(n_p, D_MODEL)
    y_s = jnp.transpose(x_sample, (1, 0, 2)).reshape(t_s * bs, D_MODEL)
    pad_new = 2 * SUBLANES - t_s
    kv_all = [jnp.zeros((depth, bp, ATTN_WIDTH, t_p), F32) for _ in range(2)]

    outs = [[] for _ in range(6)]
    for l in range(depth):
        q, u, *kv_all = _inproj_prompt(y_p, l, bp, in_weights, k_gain_t, gsum, tm_p, kv_all)
        attn = _prompt_attn(slopes, q.reshape(bp, t_p, ATTN_WIDTH), kv_all[0], kv_all[1], l)
        y_p, cs_p = _prompt_ffn(y_p, attn.reshape(n_p, ATTN_WIDTH), u, l, ffn_weights, bp, tm_p)
        outs[0].append(u.reshape(bp, t_p, POOL_WIDTH)[:, t_p - POOL_BUF:, :])
        cs_p = cs_p[:, :, :, SUBLANES - (CONV_WIDTH - 1):, :]
        outs[1].append(jnp.transpose(cs_p, (0, 3, 1, 2, 4)).reshape(bp, CONV_WIDTH - 1, 2 * D_FF))

        qs, ks, vs, us = _inproj_rows(y_s, l, in_weights, k_gain, gsum)
        to_seq = lambda a: jnp.transpose(a.reshape(t_s, bs, ATTN_WIDTH), (1, 0, 2))
        qs_b, ks_b, vs_b, us_b = to_seq(qs), to_seq(ks), to_seq(vs), to_seq(us)
        q_rep = jnp.repeat(qs_b, N_HEADS, axis=1)
        k_new = jnp.pad(ks_b, ((0, 0), (0, pad_new), (0, 0)))
        v_new = jnp.pad(vs_b, ((0, 0), (0, pad_new), (0, 0)))
        attn_s = _decode_attn(page_table + l * n_phys, slope_rows, q_rep, k_new, v_new,
                              cache_kt, cache_vt, past_len, t_s)
        attn_s = jnp.transpose(attn_s, (1, 0, 2)).reshape(t_s * bs, ATTN_WIDTH).astype(BF16)
        y_s, cs_s = _sample_ffn(y_s, attn_s, us, pool_state_tm, conv_state_c, l, ffn_weights, bs)
        outs[2].append(ks_b.reshape(bs, t_s, N_HEADS, HEAD_DIM))
        outs[3].append(vs_b.reshape(bs, t_s, N_HEADS, HEAD_DIM))
        outs[4].append(jnp.concatenate([state_pool[l], us_b], axis=1)[:, t_s:, :])
        cs_s = cs_s.reshape(2, N_FF_CHUNKS, CONV_WIDTH - 1, bs, FF_CHUNK)
        outs[5].append(jnp.transpose(cs_s, (3, 2, 0, 1, 4)).reshape(bs, CONV_WIDTH - 1, 2 * D_FF))

    y_prompt = y_p.reshape(bp, t_p, D_MODEL)
    y_sample = jnp.transpose(y_s.reshape(t_s, bs, D_MODEL), (1, 0, 2))
    new_k_p, new_v_p = (jnp.transpose(a.reshape(depth, bp, N_HEADS, HEAD_DIM, t_p), (0, 1, 4, 2, 3))
                        for a in kv_all)
    stacked = [jnp.stack(o) for o in outs]
    return (y_prompt, y_sample, new_k_p, new_v_p) + tuple(stacked)
```

```python
import functools

import numpy as np
import jax
import jax.numpy as jnp
from jax import lax
from jax.experimental import pallas as pl
from jax.experimental.pallas import tpu as pltpu

F32 = jnp.float32
BF16 = jnp.bfloat16
I32 = jnp.int32

D_MODEL = 1024
HEAD_DIM = 64
N_HEADS = 8
ATTN_WIDTH = N_HEADS * HEAD_DIM
POOL_WIDTH = 512
MOBA_BLOCK = 256
MOBA_TOPK = 3
POOL_WINDOWS = (2, 4, 8, 16)
POOL_GROUP = 128
POOL_BUF = 15
D_FF = 2816
CONV_WIDTH = 3
RMS_EPS = 1e-6
PAGE_SIZE = 128
PAGES_PER_BLOCK = MOBA_BLOCK // PAGE_SIZE

LANES = 128
SUBLANES = 8
FF_CHUNK = 256
N_FF_CHUNKS = D_FF // FF_CHUNK
POOL_CARRY = 16
NEG_BIG = -1e30
QK_SCALE = HEAD_DIM ** -0.5
DEC_GROUP = 8
DEC_RING = 4
VMEM_LIMIT = 56 * 1024 * 1024

_NT = (((1,), (1,)), ((), ()))


def _rms(x, gain):
    ms = jnp.mean(x * x, axis=-1, keepdims=True)
    return x * lax.rsqrt(ms + RMS_EPS) * gain


def _head_rms_rows(blk, gsum, gain):
    sq = blk * blk
    hi = sq.astype(BF16)
    lo = (sq - hi.astype(F32)).astype(BF16)
    ssum = jnp.dot(jnp.concatenate([hi, lo], axis=1), gsum, preferred_element_type=F32)
    return blk * lax.rsqrt(ssum * (1.0 / HEAD_DIM) + RMS_EPS) * gain


def _inproj_prompt_kernel(x_ref, g_ref, wqu_ref, wkvt_ref, qg_ref, kgt_ref, gsum_ref, *rest):
    q_ref, u_ref, kt_ref, vt_ref = rest[-4:]
    tm = x_ref.shape[0]
    h = _rms(x_ref[...], g_ref[...]).astype(BF16)
    qu = jnp.dot(h, wqu_ref[...], preferred_element_type=F32)
    kvt = lax.dot_general(wkvt_ref[...], h, _NT, preferred_element_type=F32)
    gsum = gsum_ref[...]
    for j in range(ATTN_WIDTH // LANES):
        cols = slice(j * LANES, (j + 1) * LANES)
        q_ref[:, cols] = _head_rms_rows(qu[:, cols], gsum, qg_ref[:, cols])
    u_ref[...] = qu[:, ATTN_WIDTH:]
    gain_t = jnp.concatenate([kgt_ref[...]] * (tm // LANES), axis=1)
    for hd in range(N_HEADS):
        rows = slice(hd * HEAD_DIM, (hd + 1) * HEAD_DIM)
        blk = kvt[rows, :]
        ms = jnp.sum(blk * blk, axis=0, keepdims=True) * (1.0 / HEAD_DIM)
        kt_ref[rows, :] = blk * lax.rsqrt(ms + RMS_EPS) * gain_t
    vt_ref[...] = kvt[ATTN_WIDTH:, :]


def _inproj_rows_kernel(x_ref, g_ref, wqu_ref, wkvt_ref, qg_ref, kg_ref, gsum_ref, q_ref, k_ref, v_ref, u_ref):
    h = _rms(x_ref[...], g_ref[...]).astype(BF16)
    qu = jnp.dot(h, wqu_ref[...], preferred_element_type=F32)
    kv = lax.dot_general(h, wkvt_ref[...], _NT, preferred_element_type=F32)
    gsum = gsum_ref[...]
    for j in range(ATTN_WIDTH // LANES):
        cols = slice(j * LANES, (j + 1) * LANES)
        q_ref[:, cols] = _head_rms_rows(qu[:, cols], gsum, qg_ref[:, cols])
        k_ref[:, cols] = _head_rms_rows(kv[:, cols], gsum, kg_ref[:, cols])
    v_ref[...] = kv[:, ATTN_WIDTH:]
    u_ref[...] = qu[:, ATTN_WIDTH:]


def _inproj_weight_specs(layer):
    lay = lambda shape, **kw: pl.BlockSpec((None,) + shape, lambda i: (layer,) + (0,) * len(shape), **kw)
    return [lay((1, D_MODEL)),
            lay((D_MODEL, ATTN_WIDTH + POOL_WIDTH), pipeline_mode=pl.Buffered(1)),
            lay((2 * ATTN_WIDTH, D_MODEL), pipeline_mode=pl.Buffered(1)),
            lay((1, ATTN_WIDTH))]


def _inproj_prompt(x, layer, n_batch, weights, k_gain_t, gsum, tm, kv_all):
    n = x.shape[0]
    kt_all, vt_all = kv_all
    tiles = n // n_batch // tm
    row = lambda i: (i, 0)
    kv_spec = pl.BlockSpec((None, None, ATTN_WIDTH, tm), lambda i: (layer, i // tiles, 0, i % tiles))
    in_specs = [pl.BlockSpec((tm, D_MODEL), row)] + _inproj_weight_specs(layer) + [
        pl.BlockSpec((None, HEAD_DIM, LANES), lambda i: (layer, 0, 0)),
        pl.BlockSpec((2 * LANES, LANES), lambda i: (0, 0))]
    args = [x, *weights, k_gain_t, gsum]
    aliases = {len(args): 2, len(args) + 1: 3}
    in_specs += [pl.BlockSpec(memory_space=pl.ANY)] * 2
    args += [kt_all, vt_all]
    rows_out = jax.ShapeDtypeStruct((n, ATTN_WIDTH), F32)
    kv_out = jax.ShapeDtypeStruct(kt_all.shape, F32)
    return pl.pallas_call(
        _inproj_prompt_kernel,
        grid=(n // tm,),
        in_specs=in_specs,
        out_specs=[pl.BlockSpec((tm, ATTN_WIDTH), row)] * 2 + [kv_spec] * 2,
        out_shape=[rows_out, rows_out, kv_out, kv_out],
        input_output_aliases=aliases,
        compiler_params=pltpu.CompilerParams(dimension_semantics=("arbitrary",), vmem_limit_bytes=VMEM_LIMIT),
        name="inproj_prompt",
    )(*args)


def _inproj_rows(x, layer, weights, k_gain, gsum):
    n = x.shape[0]
    full = lambda shape: pl.BlockSpec(shape, lambda i: (0,) * len(shape))
    out = jax.ShapeDtypeStruct((n, ATTN_WIDTH), F32)
    return pl.pallas_call(
        _inproj_rows_kernel,
        grid=(1,),
        in_specs=[full((n, D_MODEL))] + _inproj_weight_specs(layer) + [
            pl.BlockSpec((None, 1, ATTN_WIDTH), lambda i: (layer, 0, 0)), full((2 * LANES, LANES))],
        out_specs=[full((n, ATTN_WIDTH))] * 4,
        out_shape=[out] * 4,
        compiler_params=pltpu.CompilerParams(dimension_semantics=("arbitrary",), vmem_limit_bytes=VMEM_LIMIT),
        name="inproj_rows",
    )(x, *weights, k_gain, gsum)


def _prompt_attn_kernel(slopes_ref, q_ref, kt_ref, vt_ref, o_ref, qa_ref, ka_ref, vb_ref, s_ref):
    t_len = q_ref.shape[1]
    nb = t_len // MOBA_BLOCK
    pair = pl.program_id(1)
    kt = kt_ref[...]
    lane = lax.broadcasted_iota(I32, (1, LANES), 1)
    in_h0 = lane < HEAD_DIM

    a_idx = lax.broadcasted_iota(I32, (HEAD_DIM, t_len), 0)
    pos = lax.broadcasted_iota(I32, (HEAD_DIM, t_len), 1)
    k_aux = jnp.where(a_idx < nb, jnp.where(pos // MOBA_BLOCK == a_idx, 1.0, 0.0),
                      jnp.where(a_idx == nb, (pos % MOBA_BLOCK).astype(F32), 0.0))
    ka_ref[0] = jnp.concatenate([kt[:HEAD_DIM], k_aux], axis=0).astype(BF16)
    ka_ref[1] = jnp.concatenate([k_aux, kt[HEAD_DIM:]], axis=0).astype(BF16)
    vb_ref[...] = vt_ref[...].astype(BF16)

    def write_q_aug(t0, n_q, gate):
        q_blk = (t0 + lax.broadcasted_iota(I32, (nb, n_q), 1)) // MOBA_BLOCK
        n_idx = lax.broadcasted_iota(I32, (nb, n_q), 0)
        aux_rows = []
        for hh in range(2):
            slope = slopes_ref[2 * pair + hh]
            if gate is None:
                selected = n_idx <= q_blk
            else:
                g = gate[hh * nb:(hh + 1) * nb]
                rank = jnp.zeros((nb, n_q), F32)
                for m in range(nb):
                    gm = g[m:m + 1, :]
                    beats = jnp.where(gm > g, 1.0, jnp.where(gm == g, jnp.where(m < n_idx, 1.0, 0.0), 0.0))
                    rank = rank + jnp.where(m < q_blk, beats, 0.0)
                thresh = jnp.where(n_idx < q_blk, float(MOBA_TOPK),
                                   jnp.where(n_idx == q_blk, float(nb + 1), -1.0))
                selected = rank < thresh
            bias = jnp.where(selected, slope * (MOBA_BLOCK * n_idx).astype(F32), NEG_BIG)
            slope_rows = jnp.where(n_idx == 0, slope, 0.0)
            aux_rows.append(jnp.concatenate([bias, slope_rows], axis=0))
        pad = jnp.zeros((HEAD_DIM - 2 * nb, n_q), F32)
        aux_t = jnp.concatenate([aux_rows[1], pad, aux_rows[0], pad], axis=0).T
        qs = q_ref[0, t0:t0 + n_q, :] * QK_SCALE
        qa_ref[0, t0:t0 + n_q, :] = jnp.where(in_h0, qs, aux_t).astype(BF16)
        qa_ref[1, t0:t0 + n_q, :] = jnp.where(in_h0, aux_t, qs).astype(BF16)

    t_free = min(t_len, (MOBA_TOPK + 1) * MOBA_BLOCK)
    write_q_aug(0, t_free, None)
    if t_free < t_len:
        km_cols = jnp.zeros((LANES, LANES), F32)
        for n in range(nb):
            blk_sum = jnp.sum(kt[:, n * MOBA_BLOCK:(n + 1) * MOBA_BLOCK], axis=1, keepdims=True)
            km_cols = jnp.where(lane == n, blk_sum * (1.0 / MOBA_BLOCK), km_cols)
        kmeans = km_cols.T[:nb]
        km2 = jnp.concatenate([jnp.where(in_h0, kmeans, 0.0), jnp.where(in_h0, 0.0, kmeans)], axis=0)
        q = q_ref[0, t_free:, :]
        km_hi = km2.astype(BF16)
        km_lo = (km2 - km_hi.astype(F32)).astype(BF16)
        q_hi = q.astype(BF16)
        q_lo = (q - q_hi.astype(F32)).astype(BF16)
        g_hi = lax.dot_general(jnp.concatenate([km_hi, km_lo], axis=0), q_hi, _NT, preferred_element_type=F32)
        gate = (g_hi[:2 * nb] + g_hi[2 * nb:]) + lax.dot_general(km_hi, q_lo, _NT, preferred_element_type=F32)
        write_q_aug(t_free, t_len - t_free, gate)

    r_i = lax.broadcasted_iota(I32, (MOBA_BLOCK, MOBA_BLOCK), 0)
    c_i = lax.broadcasted_iota(I32, (MOBA_BLOCK, MOBA_BLOCK), 1)
    causal = c_i <= r_i
    back = list(range(nb - 1, -1, -1))
    half = (nb + 1) // 2
    order = [i for pair in zip(back[:half], back[half:] + [None]) for i in pair if i is not None]
    for i in order:
        r0 = i * MOBA_BLOCK
        outs = []
        for hh in range(2):
            qa = qa_ref[hh, r0:r0 + MOBA_BLOCK, :]
            m_part = jnp.full((MOBA_BLOCK, LANES), NEG_BIG, F32)
            for j in range(i + 1):
                keys = slice(j * MOBA_BLOCK, (j + 1) * MOBA_BLOCK)
                s = jnp.dot(qa, ka_ref[hh, :, keys], preferred_element_type=F32)
                if j == i:
                    s = jnp.where(causal, s, NEG_BIG)
                s_ref[hh, :, keys] = s
                m_part = jnp.maximum(m_part, jnp.maximum(s[:, :LANES], s[:, LANES:]))
            m = jnp.broadcast_to(jnp.max(m_part, axis=1, keepdims=True), (MOBA_BLOCK, LANES))
            l_part = jnp.zeros((MOBA_BLOCK, LANES), F32)
            acc = jnp.zeros((MOBA_BLOCK, LANES), F32)
            for j in range(i + 1):
                keys = slice(j * MOBA_BLOCK, (j + 1) * MOBA_BLOCK)
                s = s_ref[hh, :, keys]
                p0 = jnp.exp(s[:, :LANES] - m)
                p1 = jnp.exp(s[:, LANES:] - m)
                l_part = l_part + (p0 + p1)
                p = jnp.concatenate([p0, p1], axis=1).astype(BF16)
                acc = acc + lax.dot_general(p, vb_ref[:, keys], _NT, preferred_element_type=F32)
            outs.append(acc / jnp.sum(l_part, axis=1, keepdims=True))
        o_ref[0, r0:r0 + MOBA_BLOCK, :] = jnp.where(in_h0, outs[0], outs[1]).astype(o_ref.dtype)


def _prompt_attn(slopes, q, kt_all, vt_all, layer):
    b, t_len, _ = q.shape
    q_blk = pl.BlockSpec((1, t_len, LANES), lambda i, j: (i, 0, j))
    kv_blk = pl.BlockSpec((None, None, LANES, t_len), lambda i, j: (layer, i, j, 0))
    return pl.pallas_call(
        _prompt_attn_kernel,
        grid=(b, ATTN_WIDTH // LANES),
        in_specs=[pl.BlockSpec(memory_space=pltpu.SMEM), q_blk, kv_blk, kv_blk],
        out_specs=q_blk,
        out_shape=jax.ShapeDtypeStruct((b, t_len, ATTN_WIDTH), BF16),
        scratch_shapes=[pltpu.VMEM((2, t_len, LANES), BF16), pltpu.VMEM((2, LANES, t_len), BF16),
                        pltpu.VMEM((LANES, t_len), BF16), pltpu.VMEM((2, MOBA_BLOCK, t_len), F32)],
        compiler_params=pltpu.CompilerParams(dimension_semantics=("arbitrary", "arbitrary"),
                                             vmem_limit_bytes=VMEM_LIMIT),
        name="prompt_attn",
    )(slopes, q, kt_all, vt_all)


def _decode_attn_kernel(pt_ref, slope_ref, qrep_ref, knew_ref, vnew_ref, ck_ref, cv_ref, o_ref,
                        buf_ref, sem, s_ref, g_ref, *, past_len, n_new):
    n_seq = qrep_ref.shape[0]
    n_rows = qrep_ref.shape[1]
    n_blk = past_len // MOBA_BLOCK
    grp = buf_ref.shape[1] // PAGES_PER_BLOCK
    grp_pages = buf_ref.shape[1]
    n_grp = n_blk // grp
    per_seq = 2 * n_grp
    n_chunks = n_seq * per_seq
    n_pad = knew_ref.shape[1]

    def copies(src_ref, pages, slot):
        return [pltpu.make_async_copy(src_ref.at[page], buf_ref.at[slot, h], sem.at[slot])
                for h, page in enumerate(pages)]

    def start_chunk(g):
        seq = g // per_seq
        c = g % per_seq
        first_page = (c % n_grp) * grp_pages
        slot = g % DEC_RING
        pages = [pt_ref[seq, first_page + h] for h in range(grp_pages)]

        @pl.when(c < n_grp)
        def _():
            for cp in copies(ck_ref, pages, slot):
                cp.start()

        @pl.when(c >= n_grp)
        def _():
            for cp in copies(cv_ref, pages, slot):
                cp.start()

    def advance(g):
        slot = g % DEC_RING
        for cp in copies(ck_ref, [0] * grp_pages, slot):
            cp.wait()
        nxt = g + DEC_RING - 1

        @pl.when(nxt < n_chunks)
        def _():
            start_chunk(nxt)
        return slot

    for g0 in range(DEC_RING - 1):
        start_chunk(jnp.int32(g0))

    row = lax.broadcasted_iota(I32, (n_rows, 1), 0)
    h_row = row % N_HEADS
    t_row = row // N_HEADS
    head_mask = (lax.broadcasted_iota(I32, (1, ATTN_WIDTH), 1) // HEAD_DIM) == h_row
    slope = slope_ref[...]
    key_lane = lax.broadcasted_iota(I32, (1, MOBA_BLOCK), 1)
    alibi0 = -(slope * (past_len + t_row - key_lane).astype(F32))
    new_idx = lax.broadcasted_iota(I32, (1, n_pad), 1)
    own_ok = jnp.logical_and(new_idx <= t_row, new_idx < n_new)
    own_alibi = -(slope * (t_row - new_idx).astype(F32))

    def seq_body(seq, carry):
        qs = jnp.where(head_mask, qrep_ref[seq], 0.0) * QK_SCALE
        q_hi = qs.astype(BF16)
        q_lo = (qs - q_hi.astype(F32)).astype(BF16)
        q2 = jnp.concatenate([q_hi, q_lo], axis=0)

        def k_body(i, c):
            slot = advance(seq * per_seq + i)
            for b in range(grp):
                n = i * grp + b
                g = jnp.zeros((n_rows, PAGE_SIZE), F32)
                for half in range(PAGES_PER_BLOCK):
                    kf = buf_ref[slot, b * PAGES_PER_BLOCK + half]
                    k_hi = kf.astype(BF16)
                    k_lo = (kf - k_hi.astype(F32)).astype(BF16)
                    s2 = jnp.dot(q2, k_hi, preferred_element_type=F32)
                    s_hl = jnp.dot(q_hi, k_lo, preferred_element_type=F32)
                    s_ref[n, :, half * PAGE_SIZE:(half + 1) * PAGE_SIZE] = s2[:n_rows]
                    g = g + (s2[:n_rows] + s2[n_rows:] + s_hl)
                g_ref[n] = g
            return c
        lax.fori_loop(0, n_grp, k_body, 0)

        cur = [jnp.sum(g_ref[n], axis=1, keepdims=True) for n in range(n_blk)]
        sel = [jnp.zeros((n_rows, 1), F32)] * n_blk
        for _ in range(min(MOBA_TOPK, n_blk)):
            mx = functools.reduce(jnp.maximum, cur)
            first = functools.reduce(jnp.minimum, [jnp.where(c == mx, float(n), float(n_blk))
                                                   for n, c in enumerate(cur)])
            sel = [jnp.where(first == float(n), 1.0, s) for n, s in enumerate(sel)]
            cur = [jnp.where(first == float(n), -jnp.inf, c) for n, c in enumerate(cur)]

        m_run = jnp.full((n_rows, MOBA_BLOCK), NEG_BIG, F32)
        for n in range(n_blk):
            sb = jnp.where(sel[n] > 0.0, s_ref[n] + (alibi0 + slope * float(MOBA_BLOCK * n)), NEG_BIG)
            s_ref[n] = sb
            m_run = jnp.maximum(m_run, sb)

        s_own = lax.dot_general(q_hi, knew_ref[seq].astype(BF16), _NT, preferred_element_type=F32)
        s_own = jnp.where(own_ok, s_own + own_alibi, NEG_BIG)
        m = jnp.maximum(jnp.max(m_run, axis=1, keepdims=True), jnp.max(s_own, axis=1, keepdims=True))
        p_own = jnp.exp(s_own - m)
        l0 = jnp.sum(p_own, axis=1, keepdims=True)
        acc0 = jnp.dot(p_own.astype(BF16), vnew_ref[seq].astype(BF16), preferred_element_type=F32)

        def v_body(i, c):
            acc, l = c
            slot = advance(seq * per_seq + n_grp + i)
            for b in range(grp):
                p = jnp.exp(s_ref[i * grp + b] - m)
                l = l + jnp.sum(p, axis=1, keepdims=True)
                pb = p.astype(BF16)
                for half in range(PAGES_PER_BLOCK):
                    acc = acc + lax.dot_general(pb[:, half * PAGE_SIZE:(half + 1) * PAGE_SIZE],
                                                buf_ref[slot, b * PAGES_PER_BLOCK + half].astype(BF16), _NT,
                                                preferred_element_type=F32)
            return acc, l
        acc, l = lax.fori_loop(0, n_grp, v_body, (acc0, l0))

        o = jnp.where(head_mask, acc / l, 0.0)
        o_ref[seq] = jnp.sum(o.reshape(n_rows // N_HEADS, N_HEADS, ATTN_WIDTH), axis=1)
        return carry
    lax.fori_loop(0, n_seq, seq_body, 0)


def _decode_attn(page_ids, slope_rows, q_rep, k_new, v_new, cache_kt, cache_vt, past_len, n_new):
    n_seq, n_rows, _ = q_rep.shape
    n_blk = past_len // MOBA_BLOCK
    grp = DEC_GROUP if n_blk % DEC_GROUP == 0 else 1
    vmem = pl.BlockSpec(memory_space=pltpu.VMEM)
    return pl.pallas_call(
        functools.partial(_decode_attn_kernel, past_len=past_len, n_new=n_new),
        in_specs=[pl.BlockSpec(memory_space=pltpu.SMEM), vmem, vmem, vmem, vmem,
                  pl.BlockSpec(memory_space=pl.ANY), pl.BlockSpec(memory_space=pl.ANY)],
        out_specs=vmem,
        out_shape=jax.ShapeDtypeStruct((n_seq, n_rows // N_HEADS, ATTN_WIDTH), F32),
        scratch_shapes=[pltpu.VMEM((DEC_RING, grp * PAGES_PER_BLOCK, ATTN_WIDTH, PAGE_SIZE), F32),
                        pltpu.SemaphoreType.DMA((DEC_RING,)),
                        pltpu.VMEM((n_blk, n_rows, MOBA_BLOCK), F32),
                        pltpu.VMEM((n_blk, n_rows, PAGE_SIZE), F32)],
        compiler_params=pltpu.CompilerParams(vmem_limit_bytes=VMEM_LIMIT),
        name="decode_attn",
    )(page_ids, slope_rows, q_rep, k_new, v_new, cache_kt, cache_vt)


def _silu_gate(a, g):
    return g * (1.0 / (1.0 + jnp.exp(-g))) * a


def _prompt_ffn_kernel(x_ref, attn_ref, u_ref, wpool_ref, pscale_ref, wout_ref, fng_ref, wup_ref,
                       cw_ref, cb_ref, wdown_ref, y_ref, cs_ref,
                       ucarry_ref, upcarry_ref, h2_ref, x1_ref, up_ref, gated_ref):
    t = pl.program_id(1)
    tm = x_ref.shape[0]

    @pl.when(t == 0)
    def _():
        ucarry_ref[...] = jnp.zeros_like(ucarry_ref)
        upcarry_ref[...] = jnp.zeros_like(upcarry_ref)

    u = u_ref[...]
    ext = jnp.concatenate([ucarry_ref[...], u], axis=0)
    ucarry_ref[...] = u[tm - POOL_CARRY:, :]
    row = lax.broadcasted_iota(I32, (tm, 1), 0)
    n_seen = t * tm + row + 1
    outs = []
    for g, w in enumerate(POOL_WINDOWS):
        s = ext[:, g * POOL_GROUP:(g + 1) * POOL_GROUP]
        sh = 1
        while sh < w:
            s = s + pltpu.roll(s, sh, 0)
            sh *= 2
        cnt = jnp.minimum(w, n_seen).astype(F32)
        pooled = s[POOL_CARRY:, :] / cnt - u[:, g * POOL_GROUP:(g + 1) * POOL_GROUP]
        outs.append(jnp.dot(pooled.astype(BF16), wpool_ref[g], preferred_element_type=F32))
    pool = jnp.concatenate(outs, axis=1) * pscale_ref[...]
    mix = jnp.concatenate([attn_ref[...], pool.astype(BF16)], axis=1)
    x1 = x_ref[...] + jnp.dot(mix, wout_ref[...], preferred_element_type=F32)
    h2_ref[...] = _rms(x1, fng_ref[...]).astype(BF16)
    x1_ref[...] = x1
    row8 = lax.broadcasted_iota(I32, (SUBLANES, 1), 0)

    def up_proj(c, slot):
        h2 = h2_ref[...]
        for ag in range(2):
            up_ref[slot, ag] = jnp.dot(h2, wup_ref[ag, c], preferred_element_type=F32)

    def conv_gate(c, slot):
        halves = []
        for ag in range(2):
            up = up_ref[slot, ag]
            prev = upcarry_ref[ag, c]
            p1 = prev[SUBLANES - 1:SUBLANES, :]
            p2 = prev[SUBLANES - 2:SUBLANES - 1, :]
            r1 = pltpu.roll(up, 1, 0)
            r2 = pltpu.roll(up, 2, 0)
            top1 = jnp.where(row8 == 0, p1, r1[:SUBLANES])
            top2 = jnp.where(row8 == 0, p2, jnp.where(row8 == 1, p1, r2[:SUBLANES]))
            up1 = jnp.concatenate([top1, r1[SUBLANES:]], axis=0)
            up2 = jnp.concatenate([top2, r2[SUBLANES:]], axis=0)
            cw = cw_ref[ag, c]
            conv = cb_ref[ag, c] + cw[0:1, :] * up2 + cw[1:2, :] * up1 + cw[2:3, :] * up
            last = up[tm - SUBLANES:, :]
            upcarry_ref[ag, c] = last
            cs_ref[0, ag, c] = last
            halves.append(conv)
        gated_ref[c] = _silu_gate(halves[0], halves[1]).astype(BF16)

    up_proj(0, 0)
    for c in range(N_FF_CHUNKS - 1):
        up_proj(c + 1, (c + 1) % 2)
        conv_gate(c, c % 2)
    conv_gate(N_FF_CHUNKS - 1, (N_FF_CHUNKS - 1) % 2)
    gated = jnp.concatenate([gated_ref[c] for c in range(N_FF_CHUNKS)], axis=1)
    y_ref[...] = x1_ref[...] + jnp.dot(gated, wdown_ref[...], preferred_element_type=F32)


def _weight_specs(layer):
    def spec(shape):
        nd = len(shape)
        return pl.BlockSpec((None,) + shape, lambda *_: (layer,) + (0,) * nd, pipeline_mode=pl.Buffered(1))
    return [
        spec((len(POOL_WINDOWS), POOL_GROUP, POOL_GROUP)),
        spec((1, POOL_WIDTH)),
        spec((D_MODEL, D_MODEL)),
        spec((1, D_MODEL)),
        spec((2, N_FF_CHUNKS, D_MODEL, FF_CHUNK)),
        spec((2, N_FF_CHUNKS, SUBLANES, FF_CHUNK)),
        spec((2, N_FF_CHUNKS, 1, FF_CHUNK)),
        spec((D_FF, D_MODEL)),
    ]


def _prompt_ffn(x, attn, u, layer, weights, n_batch, tm):
    n = x.shape[0]
    tiles = n // n_batch // tm
    row = lambda b, t: (b * tiles + t, 0)
    return pl.pallas_call(
        _prompt_ffn_kernel,
        grid=(n_batch, tiles),
        in_specs=[pl.BlockSpec((tm, D_MODEL), row), pl.BlockSpec((tm, ATTN_WIDTH), row),
                  pl.BlockSpec((tm, POOL_WIDTH), row)] + _weight_specs(layer),
        out_specs=[pl.BlockSpec((tm, D_MODEL), row),
                   pl.BlockSpec((1, 2, N_FF_CHUNKS, SUBLANES, FF_CHUNK), lambda b, t: (b, 0, 0, 0, 0))],
        out_shape=[jax.ShapeDtypeStruct((n, D_MODEL), F32),
                   jax.ShapeDtypeStruct((n_batch, 2, N_FF_CHUNKS, SUBLANES, FF_CHUNK), F32)],
        scratch_shapes=[pltpu.VMEM((POOL_CARRY, POOL_WIDTH), F32),
                        pltpu.VMEM((2, N_FF_CHUNKS, SUBLANES, FF_CHUNK), F32),
                        pltpu.VMEM((tm, D_MODEL), BF16),
                        pltpu.VMEM((tm, D_MODEL), F32),
                        pltpu.VMEM((2, 2, tm, FF_CHUNK), F32),
                        pltpu.VMEM((N_FF_CHUNKS, tm, FF_CHUNK), BF16)],
        compiler_params=pltpu.CompilerParams(dimension_semantics=("arbitrary", "arbitrary"),
                                             vmem_limit_bytes=VMEM_LIMIT),
        name="prompt_ffn",
    )(x, attn, u, *weights)


def _sample_ffn_kernel(x_ref, attn_ref, u_ref, pstate_ref, cstate_ref, wpool_ref, pscale_ref, wout_ref,
                       fng_ref, wup_ref, cw_ref, cb_ref, wdown_ref, y_ref, cs_ref, h2_ref, gated_ref,
                       *, n_seq):
    n = x_ref.shape[0]
    u = u_ref[...]
    ext = jnp.concatenate([pstate_ref[...], u], axis=0)
    sums = {1: ext}
    w = 1
    while w < max(POOL_WINDOWS):
        prev = sums[w]
        sums[2 * w] = prev[w * n_seq:, :] + prev[:prev.shape[0] - w * n_seq, :]
        w *= 2
    outs = []
    for g, w in enumerate(POOL_WINDOWS):
        r0 = (POOL_BUF + 1 - w) * n_seq
        win = sums[w][r0:r0 + n, g * POOL_GROUP:(g + 1) * POOL_GROUP]
        pooled = win / float(w) - u[:, g * POOL_GROUP:(g + 1) * POOL_GROUP]
        outs.append(jnp.dot(pooled.astype(BF16), wpool_ref[g], preferred_element_type=F32))
    pool = jnp.concatenate(outs, axis=1) * pscale_ref[...]
    mix = jnp.concatenate([attn_ref[...], pool.astype(BF16)], axis=1)
    x1 = x_ref[...] + jnp.dot(mix, wout_ref[...], preferred_element_type=F32)
    h2_ref[...] = _rms(x1, fng_ref[...]).astype(BF16)

    def ff_body(c, carry):
        h2 = h2_ref[...]
        halves = []
        for ag in range(2):
            up = jnp.dot(h2, wup_ref[ag, c], preferred_element_type=F32)
            ext_up = jnp.concatenate([cstate_ref[ag, c], up], axis=0)
            cw = cw_ref[ag, c]
            conv = cb_ref[ag, c]
            for j in range(CONV_WIDTH):
                conv = conv + cw[j:j + 1, :] * ext_up[j * n_seq:j * n_seq + n, :]
            cs_ref[ag, c] = ext_up[n:, :]
            halves.append(conv)
        gated_ref[c] = _silu_gate(halves[0], halves[1]).astype(BF16)
        return carry
    lax.fori_loop(0, N_FF_CHUNKS, ff_body, 0)
    gated = jnp.concatenate([gated_ref[c] for c in range(N_FF_CHUNKS)], axis=1)
    y_ref[...] = x1 + jnp.dot(gated, wdown_ref[...], preferred_element_type=F32)


def _sample_ffn(x, attn, u, pool_state, conv_state, layer, weights, n_seq):
    n = x.shape[0]
    n_prev = (CONV_WIDTH - 1) * n_seq
    full = lambda shape: pl.BlockSpec(shape, lambda i: (0,) * len(shape))
    lay = lambda shape: pl.BlockSpec((None,) + shape, lambda i: (layer,) + (0,) * len(shape))
    return pl.pallas_call(
        functools.partial(_sample_ffn_kernel, n_seq=n_seq),
        grid=(1,),
        in_specs=[full((n, D_MODEL)), full((n, ATTN_WIDTH)), full((n, POOL_WIDTH)),
                  lay((POOL_BUF * n_seq, POOL_WIDTH)),
                  lay((2, N_FF_CHUNKS, n_prev, FF_CHUNK))] + _weight_specs(layer),
        out_specs=[full((n, D_MODEL)), full((2, N_FF_CHUNKS, n_prev, FF_CHUNK))],
        out_shape=[jax.ShapeDtypeStruct((n, D_MODEL), F32),
                   jax.ShapeDtypeStruct((2, N_FF_CHUNKS, n_prev, FF_CHUNK), F32)],
        scratch_shapes=[pltpu.VMEM((n, D_MODEL), BF16), pltpu.VMEM((N_FF_CHUNKS, n, FF_CHUNK), BF16)],
        compiler_params=pltpu.CompilerParams(dimension_semantics=("arbitrary",), vmem_limit_bytes=VMEM_LIMIT),
        name="sample_ffn",
    )(x, attn, u, pool_state, conv_state, *weights)


def _ff_cols(a):
    return a.reshape(a.shape[:-1] + (2, N_FF_CHUNKS, FF_CHUNK))


def kernel(x_prompt, x_sample, cache_k, cache_v, state_pool, state_conv, page_table, attn_norm, w_in, q_norm, k_norm, w_pool, pool_scale, w_out, ffn_norm, w_up, conv_w, conv_b, w_down):
    depth = w_in.shape[0]
    bp, t_p, _ = x_prompt.shape
    bs, t_s, _ = x_sample.shape
    n_phys = cache_k.shape[1]
    n_pages = page_table.shape[1]
    past_len = n_pages * PAGE_SIZE
    assert t_p % MOBA_BLOCK == 0 and past_len % MOBA_BLOCK == 0 and t_p >= POOL_CARRY
    assert t_s <= SUBLANES and 2 * (t_p // MOBA_BLOCK) <= HEAD_DIM

    w_qu = jnp.concatenate([w_in[:, :, :ATTN_WIDTH], w_in[:, :, 3 * ATTN_WIDTH:]], axis=2).astype(BF16)
    w_kvt = jnp.transpose(w_in[:, :, ATTN_WIDTH:3 * ATTN_WIDTH], (0, 2, 1)).astype(BF16)
    q_gain = jnp.tile(q_norm, (1, N_HEADS))[:, None, :]
    k_gain = jnp.tile(k_norm, (1, N_HEADS))[:, None, :]
    k_gain_t = jnp.broadcast_to(k_norm[:, :, None], (depth, HEAD_DIM, LANES))
    in_weights = (attn_norm[:, None, :], w_qu, w_kvt, q_gain)
    w_up_c = jnp.transpose(_ff_cols(w_up), (0, 2, 3, 1, 4)).astype(BF16)
    conv_w_c = jnp.transpose(_ff_cols(conv_w), (0, 2, 3, 1, 4))
    conv_w_c = jnp.pad(conv_w_c, ((0, 0), (0, 0), (0, 0), (0, SUBLANES - CONV_WIDTH), (0, 0)))
    conv_b_c = _ff_cols(conv_b)[:, :, :, None, :]
    ffn_weights = (w_pool.astype(BF16), pool_scale[:, None, :], w_out.astype(BF16), ffn_norm[:, None, :],
                   w_up_c, conv_w_c, conv_b_c, w_down.astype(BF16))
    head_of = np.arange(2 * LANES) % LANES // HEAD_DIM
    gsum = jnp.asarray(head_of[:, None] == head_of[None, :LANES], BF16)
    slopes_np = np.exp2(-np.arange(1, N_HEADS + 1, dtype=np.float64)).astype(np.float32)
    slopes = jnp.asarray(slopes_np)
    slope_rows = jnp.asarray(np.tile(slopes_np, t_s)[:, None])

    cache_kt = jnp.transpose(cache_k, (0, 1, 3, 4, 2)).reshape(depth * n_phys, ATTN_WIDTH, PAGE_SIZE)
    cache_vt = jnp.transpose(cache_v, (0, 1, 3, 4, 2)).reshape(depth * n_phys, ATTN_WIDTH, PAGE_SIZE)
    pool_state_tm = jnp.transpose(state_pool, (0, 2, 1, 3)).reshape(depth, POOL_BUF * bs, POOL_WIDTH)
    conv_state_c = jnp.transpose(_ff_cols(state_conv), (0, 3, 4, 2, 1, 5))
    conv_state_c = conv_state_c.reshape(depth, 2, N_FF_CHUNKS, (CONV_WIDTH - 1) * bs, FF_CHUNK)

    n_p = bp * t_p
    tm_p = 512 if t_p % 512 == 0 else MOBA_BLOCK
    y_p = x_prompt.reshape(n_p, D_MODEL)
    y_s = jnp.transpose(x_sample, (1, 0, 2)).reshape(t_s * bs, D_MODEL)
    pad_new = 2 * SUBLANES - t_s
    kv_all = [jnp.zeros((depth, bp, ATTN_WIDTH, t_p), F32) for _ in range(2)]

    outs = [[] for _ in range(6)]
    for l in range(depth):
        q, u, *kv_all = _inproj_prompt(y_p, l, bp, in_weights, k_gain_t, gsum, tm_p, kv_all)
        attn = _prompt_attn(slopes, q.reshape(bp, t_p, ATTN_WIDTH), kv_all[0], kv_all[1], l)
        y_p, cs_p = _prompt_ffn(y_p, attn.reshape(n_p, ATTN_WIDTH), u, l, ffn_weights, bp, tm_p)
        outs[0].append(u.reshape(bp, t_p, POOL_WIDTH)[:, t_p - POOL_BUF:, :])
        cs_p = cs_p[:, :, :, SUBLANES - (CONV_WIDTH - 1):, :]
        outs[1].append(jnp.transpose(cs_p, (0, 3, 1, 2, 4)).reshape(bp, CONV_WIDTH - 1, 2 * D_FF))

        qs, ks, vs, us = _inproj_rows(y_s, l, in_weights, k_gain, gsum)
        to_seq = lambda a: jnp.transpose(a.reshape(t_s, bs, ATTN_WIDTH), (1, 0, 2))
        qs_b, ks_b, vs_b, us_b = to_seq(qs), to_seq(ks), to_seq(vs), to_seq(us)
        q_rep = jnp.repeat(qs_b, N_HEADS, axis=1)
        k_new = jnp.pad(ks_b, ((0, 0), (0, pad_new), (0, 0)))
        v_new = jnp.pad(vs_b, ((0, 0), (0, pad_new), (0, 0)))
        attn_s = _decode_attn(page_table + l * n_phys, slope_rows, q_rep, k_new, v_new,
                              cache_kt, cache_vt, past_len, t_s)
        attn_s = jnp.transpose(attn_s, (1, 0, 2)).reshape(t_s * bs, ATTN_WIDTH).astype(BF16)
        y_s, cs_s = _sample_ffn(y_s, attn_s, us, pool_state_tm, conv_state_c, l, ffn_weights, bs)
        outs[2].append(ks_b.reshape(bs, t_s, N_HEADS, HEAD_DIM))
        outs[3].append(vs_b.reshape(bs, t_s, N_HEADS, HEAD_DIM))
        outs[4].append(jnp.concatenate([state_pool[l], us_b], axis=1)[:, t_s:, :])
        cs_s = cs_s.reshape(2, N_FF_CHUNKS, CONV_WIDTH - 1, bs, FF_CHUNK)
        outs[5].append(jnp.transpose(cs_s, (3, 2, 0, 1, 4)).reshape(bs, CONV_WIDTH - 1, 2 * D_FF))

    y_prompt = y_p.reshape(bp, t_p, D_MODEL)
    y_sample = jnp.transpose(y_s.reshape(t_s, bs, D_MODEL), (1, 0, 2))
    new_k_p, new_v_p = (jnp.transpose(a.reshape(depth, bp, N_HEADS, HEAD_DIM, t_p), (0, 1, 4, 2, 3))
                        for a in kv_all)
    stacked = [jnp.stack(o) for o in outs]
    return (y_prompt, y_sample, new_k_p, new_v_p) + tuple(stacked)
```

```python
import functools

import numpy as np
import jax
import jax.numpy as jnp
from jax import lax
from jax.experimental import pallas as pl
from jax.experimental.pallas import tpu as pltpu

F32 = jnp.float32
BF16 = jnp.bfloat16
I32 = jnp.int32

D_MODEL = 1024
HEAD_DIM = 64
N_HEADS = 8
ATTN_WIDTH = N_HEADS * HEAD_DIM
POOL_WIDTH = 512
MOBA_BLOCK = 256
MOBA_TOPK = 3
POOL_WINDOWS = (2, 4, 8, 16)
POOL_GROUP = 128
POOL_BUF = 15
D_FF = 2816
CONV_WIDTH = 3
RMS_EPS = 1e-6
PAGE_SIZE = 128
PAGES_PER_BLOCK = MOBA_BLOCK // PAGE_SIZE

LANES = 128
SUBLANES = 8
FF_CHUNK = 256
N_FF_CHUNKS = D_FF // FF_CHUNK
POOL_CARRY = 16
NEG_BIG = -1e30
QK_SCALE = HEAD_DIM ** -0.5
DEC_GROUP = 8
DEC_RING = 4
VMEM_LIMIT = 56 * 1024 * 1024

_NT = (((1,), (1,)), ((), ()))


def _rms(x, gain):
    ms = jnp.mean(x * x, axis=-1, keepdims=True)
    return x * lax.rsqrt(ms + RMS_EPS) * gain


def _head_rms_rows(blk, gsum, gain):
    sq = blk * blk
    hi = sq.astype(BF16)
    lo = (sq - hi.astype(F32)).astype(BF16)
    ssum = jnp.dot(jnp.concatenate([hi, lo], axis=1), gsum, preferred_element_type=F32)
    return blk * lax.rsqrt(ssum * (1.0 / HEAD_DIM) + RMS_EPS) * gain


def _inproj_prompt_kernel(x_ref, g_ref, wqu_ref, wkvt_ref, qg_ref, kgt_ref, gsum_ref, *rest):
    q_ref, u_ref, kt_ref, vt_ref = rest[-4:]
    tm = x_ref.shape[0]
    h = _rms(x_ref[...], g_ref[...]).astype(BF16)
    qu = jnp.dot(h, wqu_ref[...], preferred_element_type=F32)
    kvt = lax.dot_general(wkvt_ref[...], h, _NT, preferred_element_type=F32)
    gsum = gsum_ref[...]
    for j in range(ATTN_WIDTH // LANES):
        cols = slice(j * LANES, (j + 1) * LANES)
        q_ref[:, cols] = _head_rms_rows(qu[:, cols], gsum, qg_ref[:, cols])
    u_ref[...] = qu[:, ATTN_WIDTH:]
    gain_t = jnp.concatenate([kgt_ref[...]] * (tm // LANES), axis=1)
    for hd in range(N_HEADS):
        rows = slice(hd * HEAD_DIM, (hd + 1) * HEAD_DIM)
        blk = kvt[rows, :]
        ms = jnp.sum(blk * blk, axis=0, keepdims=True) * (1.0 / HEAD_DIM)
        kt_ref[rows, :] = blk * lax.rsqrt(ms + RMS_EPS) * gain_t
    vt_ref[...] = kvt[ATTN_WIDTH:, :]


def _inproj_rows_kernel(x_ref, g_ref, wqu_ref, wkvt_ref, qg_ref, kg_ref, gsum_ref, q_ref, k_ref, v_ref, u_ref):
    h = _rms(x_ref[...], g_ref[...]).astype(BF16)
    qu = jnp.dot(h, wqu_ref[...], preferred_element_type=F32)
    kv = lax.dot_general(h, wkvt_ref[...], _NT, preferred_element_type=F32)
    gsum = gsum_ref[...]
    for j in range(ATTN_WIDTH // LANES):
        cols = slice(j * LANES, (j + 1) * LANES)
        q_ref[:, cols] = _head_rms_rows(qu[:, cols], gsum, qg_ref[:, cols])
        k_ref[:, cols] = _head_rms_rows(kv[:, cols], gsum, kg_ref[:, cols])
    v_ref[...] = kv[:, ATTN_WIDTH:]
    u_ref[...] = qu[:, ATTN_WIDTH:]


def _inproj_weight_specs(layer):
    lay = lambda shape, **kw: pl.BlockSpec((None,) + shape, lambda i: (layer,) + (0,) * len(shape), **kw)
    return [lay((1, D_MODEL)),
            lay((D_MODEL, ATTN_WIDTH + POOL_WIDTH), pipeline_mode=pl.Buffered(1)),
            lay((2 * ATTN_WIDTH, D_MODEL), pipeline_mode=pl.Buffered(1)),
            lay((1, ATTN_WIDTH))]


def _inproj_prompt(x, layer, n_batch, weights, k_gain_t, gsum, tm, kv_all):
    n = x.shape[0]
    kt_all, vt_all = kv_all
    tiles = n // n_batch // tm
    row = lambda i: (i, 0)
    kv_spec = pl.BlockSpec((None, None, ATTN_WIDTH, tm), lambda i: (layer, i // tiles, 0, i % tiles))
    in_specs = [pl.BlockSpec((tm, D_MODEL), row)] + _inproj_weight_specs(layer) + [
        pl.BlockSpec((None, HEAD_DIM, LANES), lambda i: (layer, 0, 0)),
        pl.BlockSpec((2 * LANES, LANES), lambda i: (0, 0))]
    args = [x, *weights, k_gain_t, gsum]
    aliases = {len(args): 2, len(args) + 1: 3}
    in_specs += [pl.BlockSpec(memory_space=pl.ANY)] * 2
    args += [kt_all, vt_all]
    rows_out = jax.ShapeDtypeStruct((n, ATTN_WIDTH), F32)
    kv_out = jax.ShapeDtypeStruct(kt_all.shape, F32)
    return pl.pallas_call(
        _inproj_prompt_kernel,
        grid=(n // tm,),
        in_specs=in_specs,
        out_specs=[pl.BlockSpec((tm, ATTN_WIDTH), row)] * 2 + [kv_spec] * 2,
        out_shape=[rows_out, rows_out, kv_out, kv_out],
        input_output_aliases=aliases,
        compiler_params=pltpu.CompilerParams(dimension_semantics=("arbitrary",), vmem_limit_bytes=VMEM_LIMIT),
        name="inproj_prompt",
    )(*args)


def _inproj_rows(x, layer, weights, k_gain, gsum):
    n = x.shape[0]
    full = lambda shape: pl.BlockSpec(shape, lambda i: (0,) * len(shape))
    out = jax.ShapeDtypeStruct((n, ATTN_WIDTH), F32)
    return pl.pallas_call(
        _inproj_rows_kernel,
        grid=(1,),
        in_specs=[full((n, D_MODEL))] + _inproj_weight_specs(layer) + [
            pl.BlockSpec((None, 1, ATTN_WIDTH), lambda i: (layer, 0, 0)), full((2 * LANES, LANES))],
        out_specs=[full((n, ATTN_WIDTH))] * 4,
        out_shape=[out] * 4,
        compiler_params=pltpu.CompilerParams(dimension_semantics=("arbitrary",), vmem_limit_bytes=VMEM_LIMIT),
        name="inproj_rows",
    )(x, *weights, k_gain, gsum)


def _prompt_attn_kernel(slopes_ref, q_ref, kt_ref, vt_ref, o_ref, qa_ref, ka_ref, vb_ref, s_ref):
    t_len = q_ref.shape[1]
    nb = t_len // MOBA_BLOCK
    pair = pl.program_id(1)
    kt = kt_ref[...]
    lane = lax.broadcasted_iota(I32, (1, LANES), 1)
    in_h0 = lane < HEAD_DIM

    a_idx = lax.broadcasted_iota(I32, (HEAD_DIM, t_len), 0)
    pos = lax.broadcasted_iota(I32, (HEAD_DIM, t_len), 1)
    k_aux = jnp.where(a_idx < nb, jnp.where(pos // MOBA_BLOCK == a_idx, 1.0, 0.0),
                      jnp.where(a_idx == nb, (pos % MOBA_BLOCK).astype(F32), 0.0))
    ka_ref[0] = jnp.concatenate([kt[:HEAD_DIM], k_aux], axis=0).astype(BF16)
    ka_ref[1] = jnp.concatenate([k_aux, kt[HEAD_DIM:]], axis=0).astype(BF16)
    vb_ref[...] = vt_ref[...].astype(BF16)

    def write_q_aug(t0, n_q, gate):
        q_blk = (t0 + lax.broadcasted_iota(I32, (nb, n_q), 1)) // MOBA_BLOCK
        n_idx = lax.broadcasted_iota(I32, (nb, n_q), 0)
        aux_rows = []
        for hh in range(2):
            slope = slopes_ref[2 * pair + hh]
            if gate is None:
                selected = n_idx <= q_blk
            else:
                g = gate[hh * nb:(hh + 1) * nb]
                rank = jnp.zeros((nb, n_q), F32)
                for m in range(nb):
                    gm = g[m:m + 1, :]
                    beats = jnp.where(gm > g, 1.0, jnp.where(gm == g, jnp.where(m < n_idx, 1.0, 0.0), 0.0))
                    rank = rank + jnp.where(m < q_blk, beats, 0.0)
                thresh = jnp.where(n_idx < q_blk, float(MOBA_TOPK),
                                   jnp.where(n_idx == q_blk, float(nb + 1), -1.0))
                selected = rank < thresh
            bias = jnp.where(selected, slope * (MOBA_BLOCK * n_idx).astype(F32), NEG_BIG)
            slope_rows = jnp.where(n_idx == 0, slope, 0.0)
            aux_rows.append(jnp.concatenate([bias, slope_rows], axis=0))
        pad = jnp.zeros((HEAD_DIM - 2 * nb, n_q), F32)
        aux_t = jnp.concatenate([aux_rows[1], pad, aux_rows[0], pad], axis=0).T
        qs = q_ref[0, t0:t0 + n_q, :] * QK_SCALE
        qa_ref[0, t0:t0 + n_q, :] = jnp.where(in_h0, qs, aux_t).astype(BF16)
        qa_ref[1, t0:t0 + n_q, :] = jnp.where(in_h0, aux_t, qs).astype(BF16)

    t_free = min(t_len, (MOBA_TOPK + 1) * MOBA_BLOCK)
    write_q_aug(0, t_free, None)
    if t_free < t_len:
        km_cols = jnp.zeros((LANES, LANES), F32)
        for n in range(nb):
            blk_sum = jnp.sum(kt[:, n * MOBA_BLOCK:(n + 1) * MOBA_BLOCK], axis=1, keepdims=True)
            km_cols = jnp.where(lane == n, blk_sum * (1.0 / MOBA_BLOCK), km_cols)
        kmeans = km_cols.T[:nb]
        km2 = jnp.concatenate([jnp.where(in_h0, kmeans, 0.0), jnp.where(in_h0, 0.0, kmeans)], axis=0)
        q = q_ref[0, t_free:, :]
        km_hi = km2.astype(BF16)
        km_lo = (km2 - km_hi.astype(F32)).astype(BF16)
        q_hi = q.astype(BF16)
        q_lo = (q - q_hi.astype(F32)).astype(BF16)
        g_hi = lax.dot_general(jnp.concatenate([km_hi, km_lo], axis=0), q_hi, _NT, preferred_element_type=F32)
        gate = (g_hi[:2 * nb] + g_hi[2 * nb:]) + lax.dot_general(km_hi, q_lo, _NT, preferred_element_type=F32)
        write_q_aug(t_free, t_len - t_free, gate)

    r_i = lax.broadcasted_iota(I32, (MOBA_BLOCK, MOBA_BLOCK), 0)
    c_i = lax.broadcasted_iota(I32, (MOBA_BLOCK, MOBA_BLOCK), 1)
    causal = c_i <= r_i
    back = list(range(nb - 1, -1, -1))
    half = (nb + 1) // 2
    order = [i for pair in zip(back[:half], back[half:] + [None]) for i in pair if i is not None]
    for i in order:
        r0 = i * MOBA_BLOCK
        outs = []
        for hh in range(2):
            qa = qa_ref[hh, r0:r0 + MOBA_BLOCK, :]
            m_part = jnp.full((MOBA_BLOCK, LANES), NEG_BIG, F32)
            for j in range(i + 1):
                keys = slice(j * MOBA_BLOCK, (j + 1) * MOBA_BLOCK)
                s = jnp.dot(qa, ka_ref[hh, :, keys], preferred_element_type=F32)
                if j == i:
                    s = jnp.where(causal, s, NEG_BIG)
                s_ref[hh, :, keys] = s
                m_part = jnp.maximum(m_part, jnp.maximum(s[:, :LANES], s[:, LANES:]))
            m = jnp.broadcast_to(jnp.max(m_part, axis=1, keepdims=True), (MOBA_BLOCK, LANES))
            l_part = jnp.zeros((MOBA_BLOCK, LANES), F32)
            acc = jnp.zeros((MOBA_BLOCK, LANES), F32)
            for j in range(i + 1):
                keys = slice(j * MOBA_BLOCK, (j + 1) * MOBA_BLOCK)
                s = s_ref[hh, :, keys]
                p0 = jnp.exp(s[:, :LANES] - m)
                p1 = jnp.exp(s[:, LANES:] - m)
                l_part = l_part + (p0 + p1)
                p = jnp.concatenate([p0, p1], axis=1).astype(BF16)
                acc = acc + lax.dot_general(p, vb_ref[:, keys], _NT, preferred_element_type=F32)
            outs.append(acc / jnp.sum(l_part, axis=1, keepdims=True))
        o_ref[0, r0:r0 + MOBA_BLOCK, :] = jnp.where(in_h0, outs[0], outs[1]).astype(o_ref.dtype)


def _prompt_attn(slopes, q, kt_all, vt_all, layer):
    b, t_len, _ = q.shape
    q_blk = pl.BlockSpec((1, t_len, LANES), lambda i, j: (i, 0, j))
    kv_blk = pl.BlockSpec((None, None, LANES, t_len), lambda i, j: (layer, i, j, 0))
    return pl.pallas_call(
        _prompt_attn_kernel,
        grid=(b, ATTN_WIDTH // LANES),
        in_specs=[pl.BlockSpec(memory_space=pltpu.SMEM), q_blk, kv_blk, kv_blk],
        out_specs=q_blk,
        out_shape=jax.ShapeDtypeStruct((b, t_len, ATTN_WIDTH), BF16),
        scratch_shapes=[pltpu.VMEM((2, t_len, LANES), BF16), pltpu.VMEM((2, LANES, t_len), BF16),
                        pltpu.VMEM((LANES, t_len), BF16), pltpu.VMEM((2, MOBA_BLOCK, t_len), F32)],
        compiler_params=pltpu.CompilerParams(dimension_semantics=("arbitrary", "arbitrary"),
                                             vmem_limit_bytes=VMEM_LIMIT),
        name="prompt_attn",
    )(slopes, q, kt_all, vt_all)


def _decode_attn_kernel(pt_ref, slope_ref, qrep_ref, knew_ref, vnew_ref, ck_ref, cv_ref, o_ref,
                        buf_ref, sem, s_ref, g_ref, *, past_len, n_new):
    n_seq = qrep_ref.shape[0]
    n_rows = qrep_ref.shape[1]
    n_blk = past_len // MOBA_BLOCK
    grp = buf_ref.shape[1] // PAGES_PER_BLOCK
    grp_pages = buf_ref.shape[1]
    n_grp = n_blk // grp
    per_seq = 2 * n_grp
    n_chunks = n_seq * per_seq
    n_pad = knew_ref.shape[1]

    def copies(src_ref, pages, slot):
        return [pltpu.make_async_copy(src_ref.at[page], buf_ref.at[slot, h], sem.at[slot])
                for h, page in enumerate(pages)]

    def start_chunk(g):
        seq = g // per_seq
        c = g % per_seq
        first_page = (c % n_grp) * grp_pages
        slot = g % DEC_RING
        pages = [pt_ref[seq, first_page + h] for h in range(grp_pages)]

        @pl.when(c < n_grp)
        def _():
            for cp in copies(ck_ref, pages, slot):
                cp.start()

        @pl.when(c >= n_grp)
        def _():
            for cp in copies(cv_ref, pages, slot):
                cp.start()

    def advance(g):
        slot = g % DEC_RING
        for cp in copies(ck_ref, [0] * grp_pages, slot):
            cp.wait()
        nxt = g + DEC_RING - 1

        @pl.when(nxt < n_chunks)
        def _():
            start_chunk(nxt)
        return slot

    for g0 in range(DEC_RING - 1):
        start_chunk(jnp.int32(g0))

    row = lax.broadcasted_iota(I32, (n_rows, 1), 0)
    h_row = row % N_HEADS
    t_row = row // N_HEADS
    head_mask = (lax.broadcasted_iota(I32, (1, ATTN_WIDTH), 1) // HEAD_DIM) == h_row
    slope = slope_ref[...]
    key_lane = lax.broadcasted_iota(I32, (1, MOBA_BLOCK), 1)
    alibi0 = -(slope * (past_len + t_row - key_lane).astype(F32))
    new_idx = lax.broadcasted_iota(I32, (1, n_pad), 1)
    own_ok = jnp.logical_and(new_idx <= t_row, new_idx < n_new)
    own_alibi = -(slope * (t_row - new_idx).astype(F32))

    def seq_body(seq, carry):
        qs = jnp.where(head_mask, qrep_ref[seq], 0.0) * QK_SCALE
        q_hi = qs.astype(BF16)
        q_lo = (qs - q_hi.astype(F32)).astype(BF16)
        q2 = jnp.concatenate([q_hi, q_lo], axis=0)

        def k_body(i, c):
            slot = advance(seq * per_seq + i)
            for b in range(grp):
                n = i * grp + b
                g = jnp.zeros((n_rows, PAGE_SIZE), F32)
                for half in range(PAGES_PER_BLOCK):
                    kf = buf_ref[slot, b * PAGES_PER_BLOCK + half]
                    k_hi = kf.astype(BF16)
                    k_lo = (kf - k_hi.astype(F32)).astype(BF16)
                    s2 = jnp.dot(q2, k_hi, preferred_element_type=F32)
                    s_hl = jnp.dot(q_hi, k_lo, preferred_element_type=F32)
                    s_ref[n, :, half * PAGE_SIZE:(half + 1) * PAGE_SIZE] = s2[:n_rows]
                    g = g + (s2[:n_rows] + s2[n_rows:] + s_hl)
                g_ref[n] = g
            return c
        lax.fori_loop(0, n_grp, k_body, 0)

        cur = [jnp.sum(g_ref[n], axis=1, keepdims=True) for n in range(n_blk)]
        sel = [jnp.zeros((n_rows, 1), F32)] * n_blk
        for _ in range(min(MOBA_TOPK, n_blk)):
            mx = functools.reduce(jnp.maximum, cur)
            first = functools.reduce(jnp.minimum, [jnp.where(c == mx, float(n), float(n_blk))
                                                   for n, c in enumerate(cur)])
            sel = [jnp.where(first == float(n), 1.0, s) for n, s in enumerate(sel)]
            cur = [jnp.where(first == float(n), -jnp.inf, c) for n, c in enumerate(cur)]

        m_run = jnp.full((n_rows, MOBA_BLOCK), NEG_BIG, F32)
        for n in range(n_blk):
            sb = jnp.where(sel[n] > 0.0, s_ref[n] + (alibi0 + slope * float(MOBA_BLOCK * n)), NEG_BIG)
            s_ref[n] = sb
            m_run = jnp.maximum(m_run, sb)

        s_own = lax.dot_general(q_hi, knew_ref[seq].astype(BF16), _NT, preferred_element_type=F32)
        s_own = jnp.where(own_ok, s_own + own_alibi, NEG_BIG)
        m = jnp.maximum(jnp.max(m_run, axis=1, keepdims=True), jnp.max(s_own, axis=1, keepdims=True))
        p_own = jnp.exp(s_own - m)
        l0 = jnp.sum(p_own, axis=1, keepdims=True)
        acc0 = jnp.dot(p_own.astype(BF16), vnew_ref[seq].astype(BF16), preferred_element_type=F32)

        def v_body(i, c):
            acc, l = c
            slot = advance(seq * per_seq + n_grp + i)
            for b in range(grp):
                p = jnp.exp(s_ref[i * grp + b] - m)
                l = l + jnp.sum(p, axis=1, keepdims=True)
                pb = p.astype(BF16)
                for half in range(PAGES_PER_BLOCK):
                    acc = acc + lax.dot_general(pb[:, half * PAGE_SIZE:(half + 1) * PAGE_SIZE],
                                                buf_ref[slot, b * PAGES_PER_BLOCK + half].astype(BF16), _NT,
                                                preferred_element_type=F32)
            return acc, l
        acc, l = lax.fori_loop(0, n_grp, v_body, (acc0, l0))

        o = jnp.where(head_mask, acc / l, 0.0)
        o_ref[seq] = jnp.sum(o.reshape(n_rows // N_HEADS, N_HEADS, ATTN_WIDTH), axis=1)
        return carry
    lax.fori_loop(0, n_seq, seq_body, 0)


def _decode_attn(page_ids, slope_rows, q_rep, k_new, v_new, cache_kt, cache_vt, past_len, n_new):
    n_seq, n_rows, _ = q_rep.shape
    n_blk = past_len // MOBA_BLOCK
    grp = DEC_GROUP if n_blk % DEC_GROUP == 0 else 1
    vmem = pl.BlockSpec(memory_space=pltpu.VMEM)
    return pl.pallas_call(
        functools.partial(_decode_attn_kernel, past_len=past_len, n_new=n_new),
        in_specs=[pl.BlockSpec(memory_space=pltpu.SMEM), vmem, vmem, vmem, vmem,
                  pl.BlockSpec(memory_space=pl.ANY), pl.BlockSpec(memory_space=pl.ANY)],
        out_specs=vmem,
        out_shape=jax.ShapeDtypeStruct((n_seq, n_rows // N_HEADS, ATTN_WIDTH), F32),
        scratch_shapes=[pltpu.VMEM((DEC_RING, grp * PAGES_PER_BLOCK, ATTN_WIDTH, PAGE_SIZE), F32),
                        pltpu.SemaphoreType.DMA((DEC_RING,)),
                        pltpu.VMEM((n_blk, n_rows, MOBA_BLOCK), F32),
                        pltpu.VMEM((n_blk, n_rows, PAGE_SIZE), F32)],
        compiler_params=pltpu.CompilerParams(vmem_limit_bytes=VMEM_LIMIT),
        name="decode_attn",
    )(page_ids, slope_rows, q_rep, k_new, v_new, cache_kt, cache_vt)


def _silu_gate(a, g):
    return g * (1.0 / (1.0 + jnp.exp(-g))) * a


def _prompt_ffn_kernel(x_ref, attn_ref, u_ref, wpool_ref, pscale_ref, wout_ref, fng_ref, wup_ref,
                       cw_ref, cb_ref, wdown_ref, y_ref, cs_ref,
                       ucarry_ref, upcarry_ref, h2_ref, x1_ref, up_ref, gated_ref):
    t = pl.program_id(1)
    tm = x_ref.shape[0]

    @pl.when(t == 0)
    def _():
        ucarry_ref[...] = jnp.zeros_like(ucarry_ref)
        upcarry_ref[...] = jnp.zeros_like(upcarry_ref)

    u = u_ref[...]
    ext = jnp.concatenate([ucarry_ref[...], u], axis=0)
    ucarry_ref[...] = u[tm - POOL_CARRY:, :]
    row = lax.broadcasted_iota(I32, (tm, 1), 0)
    n_seen = t * tm + row + 1
    outs = []
    for g, w in enumerate(POOL_WINDOWS):
        s = ext[:, g * POOL_GROUP:(g + 1) * POOL_GROUP]
        sh = 1
        while sh < w:
            s = s + pltpu.roll(s, sh, 0)
            sh *= 2
        cnt = jnp.minimum(w, n_seen).astype(F32)
        pooled = s[POOL_CARRY:, :] / cnt - u[:, g * POOL_GROUP:(g + 1) * POOL_GROUP]
        outs.append(jnp.dot(pooled.astype(BF16), wpool_ref[g], preferred_element_type=F32))
    pool = jnp.concatenate(outs, axis=1) * pscale_ref[...]
    mix = jnp.concatenate([attn_ref[...], pool.astype(BF16)], axis=1)
    x1 = x_ref[...] + jnp.dot(mix, wout_ref[...], preferred_element_type=F32)
    h2_ref[...] = _rms(x1, fng_ref[...]).astype(BF16)
    x1_ref[...] = x1
    row8 = lax.broadcasted_iota(I32, (SUBLANES, 1), 0)

    def ff_cols(ag, c):
        c0 = ag * D_FF + c * FF_CHUNK
        return slice(c0, c0 + FF_CHUNK)

    def up_proj(c, slot):
        h2 = h2_ref[...]
        for ag in range(2):
            up_ref[slot, ag] = jnp.dot(h2, wup_ref[:, ff_cols(ag, c)], preferred_element_type=F32)

    def conv_gate(c, slot):
        halves = []
        for ag in range(2):
            up = up_ref[slot, ag]
            cols = ff_cols(ag, c)
            prev = upcarry_ref[:, cols]
            p1 = prev[SUBLANES - 1:SUBLANES, :]
            p2 = prev[SUBLANES - 2:SUBLANES - 1, :]
            r1 = pltpu.roll(up, 1, 0)
            r2 = pltpu.roll(up, 2, 0)
            top1 = jnp.where(row8 == 0, p1, r1[:SUBLANES])
            top2 = jnp.where(row8 == 0, p2, jnp.where(row8 == 1, p1, r2[:SUBLANES]))
            up1 = jnp.concatenate([top1, r1[SUBLANES:]], axis=0)
            up2 = jnp.concatenate([top2, r2[SUBLANES:]], axis=0)
            cw = cw_ref[:, cols]
            conv = cb_ref[:, cols] + cw[0:1, :] * up2 + cw[1:2, :] * up1 + cw[2:3, :] * up
            last = up[tm - SUBLANES:, :]
            upcarry_ref[:, cols] = last
            cs_ref[0, :, cols] = last
            halves.append(conv)
        gated_ref[c] = _silu_gate(halves[0], halves[1]).astype(BF16)

    up_proj(0, 0)
    for c in range(N_FF_CHUNKS - 1):
        up_proj(c + 1, (c + 1) % 2)
        conv_gate(c, c % 2)
    conv_gate(N_FF_CHUNKS - 1, (N_FF_CHUNKS - 1) % 2)
    gated = jnp.concatenate([gated_ref[c] for c in range(N_FF_CHUNKS)], axis=1)
    y_ref[...] = x1_ref[...] + jnp.dot(gated, wdown_ref[...], preferred_element_type=F32)


def _weight_specs(layer):
    def spec(shape):
        nd = len(shape)
        return pl.BlockSpec((None,) + shape, lambda *_: (layer,) + (0,) * nd, pipeline_mode=pl.Buffered(1))
    return [
        spec((len(POOL_WINDOWS), POOL_GROUP, POOL_GROUP)),
        spec((1, POOL_WIDTH)),
        spec((D_MODEL, D_MODEL)),
        spec((1, D_MODEL)),
        spec((D_MODEL, 2 * D_FF)),
        spec((SUBLANES, 2 * D_FF)),
        spec((1, 2 * D_FF)),
        spec((D_FF, D_MODEL)),
    ]


def _prompt_ffn(x, attn, u, layer, weights, n_batch, tm):
    n = x.shape[0]
    tiles = n // n_batch // tm
    row = lambda b, t: (b * tiles + t, 0)
    return pl.pallas_call(
        _prompt_ffn_kernel,
        grid=(n_batch, tiles),
        in_specs=[pl.BlockSpec((tm, D_MODEL), row), pl.BlockSpec((tm, ATTN_WIDTH), row),
                  pl.BlockSpec((tm, POOL_WIDTH), row)] + _weight_specs(layer),
        out_specs=[pl.BlockSpec((tm, D_MODEL), row),
                   pl.BlockSpec((1, SUBLANES, 2 * D_FF), lambda b, t: (b, 0, 0))],
        out_shape=[jax.ShapeDtypeStruct((n, D_MODEL), F32),
                   jax.ShapeDtypeStruct((n_batch, SUBLANES, 2 * D_FF), F32)],
        scratch_shapes=[pltpu.VMEM((POOL_CARRY, POOL_WIDTH), F32),
                        pltpu.VMEM((SUBLANES, 2 * D_FF), F32),
                        pltpu.VMEM((tm, D_MODEL), BF16),
                        pltpu.VMEM((tm, D_MODEL), F32),
                        pltpu.VMEM((2, 2, tm, FF_CHUNK), F32),
                        pltpu.VMEM((N_FF_CHUNKS, tm, FF_CHUNK), BF16)],
        compiler_params=pltpu.CompilerParams(dimension_semantics=("arbitrary", "arbitrary"),
                                             vmem_limit_bytes=VMEM_LIMIT),
        name="prompt_ffn",
    )(x, attn, u, *weights)


def _sample_ffn_kernel(x_ref, attn_ref, u_ref, pstate_ref, cstate_ref, wpool_ref, pscale_ref, wout_ref,
                       fng_ref, wup_ref, cw_ref, cb_ref, wdown_ref, y_ref, cs_ref, h2_ref, gated_ref,
                       *, n_seq):
    n = x_ref.shape[0]
    u = u_ref[...]
    ext = jnp.concatenate([pstate_ref[...], u], axis=0)
    sums = {1: ext}
    w = 1
    while w < max(POOL_WINDOWS):
        prev = sums[w]
        sums[2 * w] = prev[w * n_seq:, :] + prev[:prev.shape[0] - w * n_seq, :]
        w *= 2
    outs = []
    for g, w in enumerate(POOL_WINDOWS):
        r0 = (POOL_BUF + 1 - w) * n_seq
        win = sums[w][r0:r0 + n, g * POOL_GROUP:(g + 1) * POOL_GROUP]
        pooled = win / float(w) - u[:, g * POOL_GROUP:(g + 1) * POOL_GROUP]
        outs.append(jnp.dot(pooled.astype(BF16), wpool_ref[g], preferred_element_type=F32))
    pool = jnp.concatenate(outs, axis=1) * pscale_ref[...]
    mix = jnp.concatenate([attn_ref[...], pool.astype(BF16)], axis=1)
    x1 = x_ref[...] + jnp.dot(mix, wout_ref[...], preferred_element_type=F32)
    h2_ref[...] = _rms(x1, fng_ref[...]).astype(BF16)

    h2 = h2_ref[...]
    for c in range(N_FF_CHUNKS):
        halves = []
        for ag in range(2):
            c0 = ag * D_FF + c * FF_CHUNK
            cols = slice(c0, c0 + FF_CHUNK)
            up = jnp.dot(h2, wup_ref[:, cols], preferred_element_type=F32)
            ext_up = jnp.concatenate([cstate_ref[:, cols], up], axis=0)
            cw = cw_ref[:, cols]
            conv = cb_ref[:, cols]
            for j in range(CONV_WIDTH):
                conv = conv + cw[j:j + 1, :] * ext_up[j * n_seq:j * n_seq + n, :]
            cs_ref[:, cols] = ext_up[n:, :]
            halves.append(conv)
        gated_ref[c] = _silu_gate(halves[0], halves[1]).astype(BF16)
    gated = jnp.concatenate([gated_ref[c] for c in range(N_FF_CHUNKS)], axis=1)
    y_ref[...] = x1 + jnp.dot(gated, wdown_ref[...], preferred_element_type=F32)


def _sample_ffn(x, attn, u, pool_state, conv_state, layer, weights, n_seq):
    n = x.shape[0]
    n_prev = (CONV_WIDTH - 1) * n_seq
    full = lambda shape: pl.BlockSpec(shape, lambda i: (0,) * len(shape))
    lay = lambda shape: pl.BlockSpec((None,) + shape, lambda i: (layer,) + (0,) * len(shape))
    return pl.pallas_call(
        functools.partial(_sample_ffn_kernel, n_seq=n_seq),
        grid=(1,),
        in_specs=[full((n, D_MODEL)), full((n, ATTN_WIDTH)), full((n, POOL_WIDTH)),
                  lay((POOL_BUF * n_seq, POOL_WIDTH)),
                  lay((n_prev, 2 * D_FF))] + _weight_specs(layer),
        out_specs=[full((n, D_MODEL)), full((n_prev, 2 * D_FF))],
        out_shape=[jax.ShapeDtypeStruct((n, D_MODEL), F32),
                   jax.ShapeDtypeStruct((n_prev, 2 * D_FF), F32)],
        scratch_shapes=[pltpu.VMEM((n, D_MODEL), BF16), pltpu.VMEM((N_FF_CHUNKS, n, FF_CHUNK), BF16)],
        compiler_params=pltpu.CompilerParams(dimension_semantics=("arbitrary",), vmem_limit_bytes=VMEM_LIMIT),
        name="sample_ffn",
    )(x, attn, u, pool_state, conv_state, *weights)


def kernel(x_prompt, x_sample, cache_k, cache_v, state_pool, state_conv, page_table, attn_norm, w_in, q_norm, k_norm, w_pool, pool_scale, w_out, ffn_norm, w_up, conv_w, conv_b, w_down):
    depth = w_in.shape[0]
    bp, t_p, _ = x_prompt.shape
    bs, t_s, _ = x_sample.shape
    n_phys = cache_k.shape[1]
    n_pages = page_table.shape[1]
    past_len = n_pages * PAGE_SIZE
    assert t_p % MOBA_BLOCK == 0 and past_len % MOBA_BLOCK == 0 and t_p >= POOL_CARRY
    assert t_s <= SUBLANES and 2 * (t_p // MOBA_BLOCK) <= HEAD_DIM

    w_qu = jnp.concatenate([w_in[:, :, :ATTN_WIDTH], w_in[:, :, 3 * ATTN_WIDTH:]], axis=2).astype(BF16)
    w_kvt = jnp.transpose(w_in[:, :, ATTN_WIDTH:3 * ATTN_WIDTH], (0, 2, 1)).astype(BF16)
    q_gain = jnp.tile(q_norm, (1, N_HEADS))[:, None, :]
    k_gain = jnp.tile(k_norm, (1, N_HEADS))[:, None, :]
    k_gain_t = jnp.broadcast_to(k_norm[:, :, None], (depth, HEAD_DIM, LANES))
    in_weights = (attn_norm[:, None, :], w_qu, w_kvt, q_gain)
    conv_w_rows = jnp.pad(conv_w, ((0, 0), (0, SUBLANES - CONV_WIDTH), (0, 0)))
    ffn_weights = (w_pool.astype(BF16), pool_scale[:, None, :], w_out.astype(BF16), ffn_norm[:, None, :],
                   w_up.astype(BF16), conv_w_rows, conv_b[:, None, :], w_down.astype(BF16))
    head_of = np.arange(2 * LANES) % LANES // HEAD_DIM
    gsum = jnp.asarray(head_of[:, None] == head_of[None, :LANES], BF16)
    slopes_np = np.exp2(-np.arange(1, N_HEADS + 1, dtype=np.float64)).astype(np.float32)
    slopes = jnp.asarray(slopes_np)
    slope_rows = jnp.asarray(np.tile(slopes_np, t_s)[:, None])

    cache_kt = jnp.transpose(cache_k, (0, 1, 3, 4, 2)).reshape(depth * n_phys, ATTN_WIDTH, PAGE_SIZE)
    cache_vt = jnp.transpose(cache_v, (0, 1, 3, 4, 2)).reshape(depth * n_phys, ATTN_WIDTH, PAGE_SIZE)
    pool_state_tm = jnp.transpose(state_pool, (0, 2, 1, 3)).reshape(depth, POOL_BUF * bs, POOL_WIDTH)
    conv_state_tm = jnp.transpose(state_conv, (0, 2, 1, 3)).reshape(depth, (CONV_WIDTH - 1) * bs, 2 * D_FF)

    n_p = bp * t_p
    tm_p = 512 if t_p % 512 == 0 else MOBA_BLOCK
    y_p = x_prompt.reshape(n_p, D_MODEL)
    y_s = jnp.transpose(x_sample, (1, 0, 2)).reshape(t_s * bs, D_MODEL)
    pad_new = 2 * SUBLANES - t_s
    page_ids = page_table[None] + (jnp.arange(depth, dtype=I32) * n_phys)[:, None, None]
    kv_all = [jnp.zeros((depth, bp, ATTN_WIDTH, t_p), F32) for _ in range(2)]

    outs = [[] for _ in range(6)]
    for l in range(depth):
        q, u, *kv_all = _inproj_prompt(y_p, l, bp, in_weights, k_gain_t, gsum, tm_p, kv_all)
        attn = _prompt_attn(slopes, q.reshape(bp, t_p, ATTN_WIDTH), kv_all[0], kv_all[1], l)
        y_p, cs_p = _prompt_ffn(y_p, attn.reshape(n_p, ATTN_WIDTH), u, l, ffn_weights, bp, tm_p)
        outs[0].append(u.reshape(bp, t_p, POOL_WIDTH)[:, t_p - POOL_BUF:, :])
        outs[1].append(cs_p[:, SUBLANES - (CONV_WIDTH - 1):, :])

        qs, ks, vs, us = _inproj_rows(y_s, l, in_weights, k_gain, gsum)
        to_seq = lambda a: jnp.transpose(a.reshape(t_s, bs, ATTN_WIDTH), (1, 0, 2))
        qs_b, ks_b, vs_b, us_b = to_seq(qs), to_seq(ks), to_seq(vs), to_seq(us)
        q_rep = jnp.repeat(qs_b, N_HEADS, axis=1)
        k_new = jnp.pad(ks_b, ((0, 0), (0, pad_new), (0, 0)))
        v_new = jnp.pad(vs_b, ((0, 0), (0, pad_new), (0, 0)))
        attn_s = _decode_attn(page_ids[l], slope_rows, q_rep, k_new, v_new,
                              cache_kt, cache_vt, past_len, t_s)
        attn_s = jnp.transpose(attn_s, (1, 0, 2)).reshape(t_s * bs, ATTN_WIDTH).astype(BF16)
        y_s, cs_s = _sample_ffn(y_s, attn_s, us, pool_state_tm, conv_state_tm, l, ffn_weights, bs)
        outs[2].append(ks_b.reshape(bs, t_s, N_HEADS, HEAD_DIM))
        outs[3].append(vs_b.reshape(bs, t_s, N_HEADS, HEAD_DIM))
        outs[4].append(jnp.concatenate([state_pool[l], us_b], axis=1)[:, t_s:, :])
        outs[5].append(jnp.transpose(cs_s.reshape(CONV_WIDTH - 1, bs, 2 * D_FF), (1, 0, 2)))

    y_prompt = y_p.reshape(bp, t_p, D_MODEL)
    y_sample = jnp.transpose(y_s.reshape(t_s, bs, D_MODEL), (1, 0, 2))
    new_k_p, new_v_p = (jnp.transpose(a.reshape(depth, bp, N_HEADS, HEAD_DIM, t_p), (0, 1, 4, 2, 3))
                        for a in kv_all)
    stacked = [jnp.stack(o) for o in outs]
    return (y_prompt, y_sample, new_k_p, new_v_p) + tuple(stacked)
```

```python
import functools

import numpy as np
import jax
import jax.numpy as jnp
from jax import lax
from jax.experimental import pallas as pl
from jax.experimental.pallas import tpu as pltpu

F32 = jnp.float32
BF16 = jnp.bfloat16
I32 = jnp.int32

D_MODEL = 1024
HEAD_DIM = 64
N_HEADS = 8
ATTN_WIDTH = N_HEADS * HEAD_DIM
POOL_WIDTH = 512
MOBA_BLOCK = 256
MOBA_TOPK = 3
POOL_WINDOWS = (2, 4, 8, 16)
POOL_GROUP = 128
POOL_BUF = 15
D_FF = 2816
CONV_WIDTH = 3
RMS_EPS = 1e-6
PAGE_SIZE = 128
PAGES_PER_BLOCK = MOBA_BLOCK // PAGE_SIZE

LANES = 128
SUBLANES = 8
FF_CHUNK = 256
N_FF_CHUNKS = D_FF // FF_CHUNK
POOL_CARRY = 16
NEG_BIG = -1e30
QK_SCALE = HEAD_DIM ** -0.5
DEC_GROUP = 8
DEC_RING = 4
VMEM_LIMIT = 56 * 1024 * 1024

_NT = (((1,), (1,)), ((), ()))
_TN = (((0,), (1,)), ((), ()))


def _rms(x, gain):
    ms = jnp.mean(x * x, axis=-1, keepdims=True)
    return x * lax.rsqrt(ms + RMS_EPS) * gain


def _head_rms_rows(blk, gsum, gain):
    sq = blk * blk
    hi = sq.astype(BF16)
    lo = (sq - hi.astype(F32)).astype(BF16)
    ssum = jnp.dot(jnp.concatenate([hi, lo], axis=1), gsum, preferred_element_type=F32)
    return blk * lax.rsqrt(ssum * (1.0 / HEAD_DIM) + RMS_EPS) * gain


def _inproj_prompt_kernel(x_ref, g_ref, wqu_ref, wkv_ref, qg_ref, kgt_ref, gsum_ref, *rest):
    q_ref, u_ref, kt_ref, vt_ref = rest[-4:]
    tm = x_ref.shape[0]
    h = _rms(x_ref[...], g_ref[...]).astype(BF16)
    qu = jnp.dot(h, wqu_ref[...], preferred_element_type=F32)
    kvt = lax.dot_general(wkv_ref[...], h, _TN, preferred_element_type=F32)
    gsum = gsum_ref[...]
    for j in range(ATTN_WIDTH // LANES):
        cols = slice(j * LANES, (j + 1) * LANES)
        q_ref[:, cols] = _head_rms_rows(qu[:, cols], gsum, qg_ref[:, cols])
    u_ref[...] = qu[:, ATTN_WIDTH:]
    gain_t = jnp.concatenate([kgt_ref[...]] * (tm // LANES), axis=1)
    for hd in range(N_HEADS):
        rows = slice(hd * HEAD_DIM, (hd + 1) * HEAD_DIM)
        blk = kvt[rows, :]
        ms = jnp.sum(blk * blk, axis=0, keepdims=True) * (1.0 / HEAD_DIM)
        kt_ref[rows, :] = blk * lax.rsqrt(ms + RMS_EPS) * gain_t
    vt_ref[...] = kvt[ATTN_WIDTH:, :]


def _inproj_rows_kernel(x_ref, g_ref, wqu_ref, wkv_ref, qg_ref, kg_ref, gsum_ref, q_ref, k_ref, v_ref, u_ref):
    h = _rms(x_ref[...], g_ref[...]).astype(BF16)
    qu = jnp.dot(h, wqu_ref[...], preferred_element_type=F32)
    kv = jnp.dot(h, wkv_ref[...], preferred_element_type=F32)
    gsum = gsum_ref[...]
    for j in range(ATTN_WIDTH // LANES):
        cols = slice(j * LANES, (j + 1) * LANES)
        q_ref[:, cols] = _head_rms_rows(qu[:, cols], gsum, qg_ref[:, cols])
        k_ref[:, cols] = _head_rms_rows(kv[:, cols], gsum, kg_ref[:, cols])
    v_ref[...] = kv[:, ATTN_WIDTH:]
    u_ref[...] = qu[:, ATTN_WIDTH:]


def _inproj_weight_specs(layer):
    lay = lambda shape, **kw: pl.BlockSpec((None,) + shape, lambda i: (layer,) + (0,) * len(shape), **kw)
    return [lay((1, D_MODEL)),
            lay((D_MODEL, ATTN_WIDTH + POOL_WIDTH), pipeline_mode=pl.Buffered(1)),
            lay((D_MODEL, 2 * ATTN_WIDTH), pipeline_mode=pl.Buffered(1)),
            lay((1, ATTN_WIDTH))]


def _inproj_prompt(x, layer, n_batch, weights, k_gain_t, gsum, tm, kv_all):
    n = x.shape[0]
    kt_all, vt_all = kv_all
    tiles = n // n_batch // tm
    row = lambda i: (i, 0)
    kv_spec = pl.BlockSpec((None, None, ATTN_WIDTH, tm), lambda i: (layer, i // tiles, 0, i % tiles))
    in_specs = [pl.BlockSpec((tm, D_MODEL), row)] + _inproj_weight_specs(layer) + [
        pl.BlockSpec((None, HEAD_DIM, LANES), lambda i: (layer, 0, 0)),
        pl.BlockSpec((2 * LANES, LANES), lambda i: (0, 0))]
    args = [x, *weights, k_gain_t, gsum]
    aliases = {len(args): 2, len(args) + 1: 3}
    in_specs += [pl.BlockSpec(memory_space=pl.ANY)] * 2
    args += [kt_all, vt_all]
    rows_out = jax.ShapeDtypeStruct((n, ATTN_WIDTH), F32)
    kv_out = jax.ShapeDtypeStruct(kt_all.shape, F32)
    return pl.pallas_call(
        _inproj_prompt_kernel,
        grid=(n // tm,),
        in_specs=in_specs,
        out_specs=[pl.BlockSpec((tm, ATTN_WIDTH), row)] * 2 + [kv_spec] * 2,
        out_shape=[rows_out, rows_out, kv_out, kv_out],
        input_output_aliases=aliases,
        compiler_params=pltpu.CompilerParams(dimension_semantics=("arbitrary",), vmem_limit_bytes=VMEM_LIMIT),
        name="inproj_prompt",
    )(*args)


def _inproj_rows(x, layer, weights, k_gain, gsum):
    n = x.shape[0]
    full = lambda shape: pl.BlockSpec(shape, lambda i: (0,) * len(shape))
    out = jax.ShapeDtypeStruct((n, ATTN_WIDTH), F32)
    return pl.pallas_call(
        _inproj_rows_kernel,
        grid=(1,),
        in_specs=[full((n, D_MODEL))] + _inproj_weight_specs(layer) + [
            pl.BlockSpec((None, 1, ATTN_WIDTH), lambda i: (layer, 0, 0)), full((2 * LANES, LANES))],
        out_specs=[full((n, ATTN_WIDTH))] * 4,
        out_shape=[out] * 4,
        compiler_params=pltpu.CompilerParams(dimension_semantics=("arbitrary",), vmem_limit_bytes=VMEM_LIMIT),
        name="inproj_rows",
    )(x, *weights, k_gain, gsum)


def _prompt_attn_kernel(slopes_ref, q_ref, kt_ref, vt_ref, o_ref, qa_ref, ka_ref, vb_ref, s_ref):
    t_len = q_ref.shape[1]
    nb = t_len // MOBA_BLOCK
    pair = pl.program_id(1)
    kt = kt_ref[...]
    lane = lax.broadcasted_iota(I32, (1, LANES), 1)
    in_h0 = lane < HEAD_DIM

    a_idx = lax.broadcasted_iota(I32, (HEAD_DIM, t_len), 0)
    pos = lax.broadcasted_iota(I32, (HEAD_DIM, t_len), 1)
    k_aux = jnp.where(a_idx < nb, jnp.where(pos // MOBA_BLOCK == a_idx, 1.0, 0.0),
                      jnp.where(a_idx == nb, (pos % MOBA_BLOCK).astype(F32), 0.0))
    ka_ref[0] = jnp.concatenate([kt[:HEAD_DIM], k_aux], axis=0).astype(BF16)
    ka_ref[1] = jnp.concatenate([k_aux, kt[HEAD_DIM:]], axis=0).astype(BF16)
    vb_ref[...] = vt_ref[...].astype(BF16)

    def write_q_aug(t0, n_q, gate):
        q_blk = (t0 + lax.broadcasted_iota(I32, (nb, n_q), 1)) // MOBA_BLOCK
        n_idx = lax.broadcasted_iota(I32, (nb, n_q), 0)
        aux_rows = []
        for hh in range(2):
            slope = slopes_ref[2 * pair + hh]
            if gate is None:
                selected = n_idx <= q_blk
            else:
                g = gate[hh * nb:(hh + 1) * nb]
                rank = jnp.zeros((nb, n_q), F32)
                for m in range(nb):
                    gm = g[m:m + 1, :]
                    beats = jnp.where(gm > g, 1.0, jnp.where(gm == g, jnp.where(m < n_idx, 1.0, 0.0), 0.0))
                    rank = rank + jnp.where(m < q_blk, beats, 0.0)
                thresh = jnp.where(n_idx < q_blk, float(MOBA_TOPK),
                                   jnp.where(n_idx == q_blk, float(nb + 1), -1.0))
                selected = rank < thresh
            bias = jnp.where(selected, slope * (MOBA_BLOCK * n_idx).astype(F32), NEG_BIG)
            slope_rows = jnp.where(n_idx == 0, slope, 0.0)
            aux_rows.append(jnp.concatenate([bias, slope_rows], axis=0))
        pad = jnp.zeros((HEAD_DIM - 2 * nb, n_q), F32)
        aux_t = jnp.concatenate([aux_rows[1], pad, aux_rows[0], pad], axis=0).T
        qs = q_ref[0, t0:t0 + n_q, :] * QK_SCALE
        qa_ref[0, t0:t0 + n_q, :] = jnp.where(in_h0, qs, aux_t).astype(BF16)
        qa_ref[1, t0:t0 + n_q, :] = jnp.where(in_h0, aux_t, qs).astype(BF16)

    t_free = min(t_len, (MOBA_TOPK + 1) * MOBA_BLOCK)
    write_q_aug(0, t_free, None)
    if t_free < t_len:
        km_cols = jnp.zeros((LANES, LANES), F32)
        for n in range(nb):
            blk_sum = jnp.sum(kt[:, n * MOBA_BLOCK:(n + 1) * MOBA_BLOCK], axis=1, keepdims=True)
            km_cols = jnp.where(lane == n, blk_sum * (1.0 / MOBA_BLOCK), km_cols)
        kmeans = km_cols.T[:nb]
        km2 = jnp.concatenate([jnp.where(in_h0, kmeans, 0.0), jnp.where(in_h0, 0.0, kmeans)], axis=0)
        q = q_ref[0, t_free:, :]
        km_hi = km2.astype(BF16)
        km_lo = (km2 - km_hi.astype(F32)).astype(BF16)
        q_hi = q.astype(BF16)
        q_lo = (q - q_hi.astype(F32)).astype(BF16)
        g_hi = lax.dot_general(jnp.concatenate([km_hi, km_lo], axis=0), q_hi, _NT, preferred_element_type=F32)
        gate = (g_hi[:2 * nb] + g_hi[2 * nb:]) + lax.dot_general(km_hi, q_lo, _NT, preferred_element_type=F32)
        write_q_aug(t_free, t_len - t_free, gate)

    r_i = lax.broadcasted_iota(I32, (MOBA_BLOCK, MOBA_BLOCK), 0)
    c_i = lax.broadcasted_iota(I32, (MOBA_BLOCK, MOBA_BLOCK), 1)
    causal = c_i <= r_i
    back = list(range(nb - 1, -1, -1))
    half = (nb + 1) // 2
    order = [i for pair in zip(back[:half], back[half:] + [None]) for i in pair if i is not None]
    for i in order:
        r0 = i * MOBA_BLOCK
        outs = []
        for hh in range(2):
            qa = qa_ref[hh, r0:r0 + MOBA_BLOCK, :]
            m_part = jnp.full((MOBA_BLOCK, LANES), NEG_BIG, F32)
            for j in range(i + 1):
                keys = slice(j * MOBA_BLOCK, (j + 1) * MOBA_BLOCK)
                s = jnp.dot(qa, ka_ref[hh, :, keys], preferred_element_type=F32)
                if j == i:
                    s = jnp.where(causal, s, NEG_BIG)
                s_ref[hh, :, keys] = s
                m_part = jnp.maximum(m_part, jnp.maximum(s[:, :LANES], s[:, LANES:]))
            m = jnp.broadcast_to(jnp.max(m_part, axis=1, keepdims=True), (MOBA_BLOCK, LANES))
            l_part = jnp.zeros((MOBA_BLOCK, LANES), F32)
            acc = jnp.zeros((MOBA_BLOCK, LANES), F32)
            for j in range(i + 1):
                keys = slice(j * MOBA_BLOCK, (j + 1) * MOBA_BLOCK)
                s = s_ref[hh, :, keys]
                p0 = jnp.exp(s[:, :LANES] - m)
                p1 = jnp.exp(s[:, LANES:] - m)
                l_part = l_part + (p0 + p1)
                p = jnp.concatenate([p0, p1], axis=1).astype(BF16)
                acc = acc + lax.dot_general(p, vb_ref[:, keys], _NT, preferred_element_type=F32)
            outs.append(acc / jnp.sum(l_part, axis=1, keepdims=True))
        o_ref[0, r0:r0 + MOBA_BLOCK, :] = jnp.where(in_h0, outs[0], outs[1]).astype(o_ref.dtype)


def _prompt_attn(slopes, q, kt_all, vt_all, layer):
    b, t_len, _ = q.shape
    q_blk = pl.BlockSpec((1, t_len, LANES), lambda i, j: (i, 0, j))
    kv_blk = pl.BlockSpec((None, None, LANES, t_len), lambda i, j: (layer, i, j, 0))
    return pl.pallas_call(
        _prompt_attn_kernel,
        grid=(b, ATTN_WIDTH // LANES),
        in_specs=[pl.BlockSpec(memory_space=pltpu.SMEM), q_blk, kv_blk, kv_blk],
        out_specs=q_blk,
        out_shape=jax.ShapeDtypeStruct((b, t_len, ATTN_WIDTH), BF16),
        scratch_shapes=[pltpu.VMEM((2, t_len, LANES), BF16), pltpu.VMEM((2, LANES, t_len), BF16),
                        pltpu.VMEM((LANES, t_len), BF16), pltpu.VMEM((2, MOBA_BLOCK, t_len), F32)],
        compiler_params=pltpu.CompilerParams(dimension_semantics=("arbitrary", "arbitrary"),
                                             vmem_limit_bytes=VMEM_LIMIT),
        name="prompt_attn",
    )(slopes, q, kt_all, vt_all)


def _decode_attn_kernel(pt_ref, slope_ref, qrep_ref, knew_ref, vnew_ref, ck_ref, cv_ref, o_ref,
                        buf_ref, sem, s_ref, g_ref, *, past_len, n_new):
    n_seq = qrep_ref.shape[0]
    n_rows = qrep_ref.shape[1]
    n_blk = past_len // MOBA_BLOCK
    grp = buf_ref.shape[1] // PAGES_PER_BLOCK
    grp_pages = buf_ref.shape[1]
    n_grp = n_blk // grp
    per_seq = 2 * n_grp
    n_chunks = n_seq * per_seq
    n_pad = knew_ref.shape[1]

    def copies(src_ref, pages, slot):
        return [pltpu.make_async_copy(src_ref.at[page], buf_ref.at[slot, h], sem.at[slot])
                for h, page in enumerate(pages)]

    def start_chunk(g):
        seq = g // per_seq
        c = g % per_seq
        first_page = (c % n_grp) * grp_pages
        slot = g % DEC_RING
        pages = [pt_ref[seq, first_page + h] for h in range(grp_pages)]

        @pl.when(c < n_grp)
        def _():
            for cp in copies(ck_ref, pages, slot):
                cp.start()

        @pl.when(c >= n_grp)
        def _():
            for cp in copies(cv_ref, pages, slot):
                cp.start()

    def advance(g):
        slot = g % DEC_RING
        for cp in copies(ck_ref, [0] * grp_pages, slot):
            cp.wait()
        nxt = g + DEC_RING - 1

        @pl.when(nxt < n_chunks)
        def _():
            start_chunk(nxt)
        return slot

    for g0 in range(DEC_RING - 1):
        start_chunk(jnp.int32(g0))

    row = lax.broadcasted_iota(I32, (n_rows, 1), 0)
    h_row = row % N_HEADS
    t_row = row // N_HEADS
    head_mask = (lax.broadcasted_iota(I32, (1, ATTN_WIDTH), 1) // HEAD_DIM) == h_row
    slope = slope_ref[...]
    key_lane = lax.broadcasted_iota(I32, (1, MOBA_BLOCK), 1)
    alibi0 = -(slope * (past_len + t_row - key_lane).astype(F32))
    new_idx = lax.broadcasted_iota(I32, (1, n_pad), 1)
    own_ok = jnp.logical_and(new_idx <= t_row, new_idx < n_new)
    own_alibi = -(slope * (t_row - new_idx).astype(F32))

    def seq_body(seq, carry):
        qs = jnp.where(head_mask, qrep_ref[seq], 0.0) * QK_SCALE
        q_hi = qs.astype(BF16)
        q_lo = (qs - q_hi.astype(F32)).astype(BF16)
        q2 = jnp.concatenate([q_hi, q_lo], axis=0)

        def k_body(i, c):
            slot = advance(seq * per_seq + i)
            for b in range(grp):
                n = i * grp + b
                g = jnp.zeros((n_rows, PAGE_SIZE), F32)
                for half in range(PAGES_PER_BLOCK):
                    kf = buf_ref[slot, b * PAGES_PER_BLOCK + half]
                    k_hi = kf.astype(BF16)
                    k_lo = (kf - k_hi.astype(F32)).astype(BF16)
                    s2 = jnp.dot(q2, k_hi, preferred_element_type=F32)
                    s_hl = jnp.dot(q_hi, k_lo, preferred_element_type=F32)
                    s_ref[n, :, half * PAGE_SIZE:(half + 1) * PAGE_SIZE] = s2[:n_rows]
                    g = g + (s2[:n_rows] + s2[n_rows:] + s_hl)
                g_ref[n] = g
            return c
        lax.fori_loop(0, n_grp, k_body, 0)

        cur = [jnp.sum(g_ref[n], axis=1, keepdims=True) for n in range(n_blk)]
        sel = [jnp.zeros((n_rows, 1), F32)] * n_blk
        for _ in range(min(MOBA_TOPK, n_blk)):
            mx = functools.reduce(jnp.maximum, cur)
            first = functools.reduce(jnp.minimum, [jnp.where(c == mx, float(n), float(n_blk))
                                                   for n, c in enumerate(cur)])
            sel = [jnp.where(first == float(n), 1.0, s) for n, s in enumerate(sel)]
            cur = [jnp.where(first == float(n), -jnp.inf, c) for n, c in enumerate(cur)]

        m_run = jnp.full((n_rows, MOBA_BLOCK), NEG_BIG, F32)
        for n in range(n_blk):
            sb = jnp.where(sel[n] > 0.0, s_ref[n] + (alibi0 + slope * float(MOBA_BLOCK * n)), NEG_BIG)
            s_ref[n] = sb
            m_run = jnp.maximum(m_run, sb)

        s_own = lax.dot_general(q_hi, knew_ref[seq].astype(BF16), _NT, preferred_element_type=F32)
        s_own = jnp.where(own_ok, s_own + own_alibi, NEG_BIG)
        m = jnp.maximum(jnp.max(m_run, axis=1, keepdims=True), jnp.max(s_own, axis=1, keepdims=True))
        p_own = jnp.exp(s_own - m)
        l0 = jnp.sum(p_own, axis=1, keepdims=True)
        acc0 = jnp.dot(p_own.astype(BF16), vnew_ref[seq].astype(BF16), preferred_element_type=F32)

        def v_body(i, c):
            acc, l = c
            slot = advance(seq * per_seq + n_grp + i)
            for b in range(grp):
                p = jnp.exp(s_ref[i * grp + b] - m)
                l = l + jnp.sum(p, axis=1, keepdims=True)
                pb = p.astype(BF16)
                for half in range(PAGES_PER_BLOCK):
                    acc = acc + lax.dot_general(pb[:, half * PAGE_SIZE:(half + 1) * PAGE_SIZE],
                                                buf_ref[slot, b * PAGES_PER_BLOCK + half].astype(BF16), _NT,
                                                preferred_element_type=F32)
            return acc, l
        acc, l = lax.fori_loop(0, n_grp, v_body, (acc0, l0))

        o = jnp.where(head_mask, acc / l, 0.0)
        o_ref[seq] = jnp.sum(o.reshape(n_rows // N_HEADS, N_HEADS, ATTN_WIDTH), axis=1)
        return carry
    lax.fori_loop(0, n_seq, seq_body, 0)


def _decode_attn(page_ids, slope_rows, q_rep, k_new, v_new, cache_kt, cache_vt, past_len, n_new):
    n_seq, n_rows, _ = q_rep.shape
    n_blk = past_len // MOBA_BLOCK
    grp = DEC_GROUP if n_blk % DEC_GROUP == 0 else 1
    vmem = pl.BlockSpec(memory_space=pltpu.VMEM)
    return pl.pallas_call(
        functools.partial(_decode_attn_kernel, past_len=past_len, n_new=n_new),
        in_specs=[pl.BlockSpec(memory_space=pltpu.SMEM), vmem, vmem, vmem, vmem,
                  pl.BlockSpec(memory_space=pl.ANY), pl.BlockSpec(memory_space=pl.ANY)],
        out_specs=vmem,
        out_shape=jax.ShapeDtypeStruct((n_seq, n_rows // N_HEADS, ATTN_WIDTH), F32),
        scratch_shapes=[pltpu.VMEM((DEC_RING, grp * PAGES_PER_BLOCK, ATTN_WIDTH, PAGE_SIZE), F32),
                        pltpu.SemaphoreType.DMA((DEC_RING,)),
                        pltpu.VMEM((n_blk, n_rows, MOBA_BLOCK), F32),
                        pltpu.VMEM((n_blk, n_rows, PAGE_SIZE), F32)],
        compiler_params=pltpu.CompilerParams(vmem_limit_bytes=VMEM_LIMIT),
        name="decode_attn",
    )(page_ids, slope_rows, q_rep, k_new, v_new, cache_kt, cache_vt)


def _silu_gate(a, g):
    return g * (1.0 / (1.0 + jnp.exp(-g))) * a


def _prompt_ffn_kernel(x_ref, attn_ref, u_ref, wpool_ref, pscale_ref, wout_ref, fng_ref, wup_ref,
                       cw_ref, cb_ref, wdown_ref, y_ref, cs_ref,
                       ucarry_ref, upcarry_ref, h2_ref, x1_ref, up_ref, gated_ref):
    t = pl.program_id(1)
    tm = x_ref.shape[0]

    @pl.when(t == 0)
    def _():
        ucarry_ref[...] = jnp.zeros_like(ucarry_ref)
        upcarry_ref[...] = jnp.zeros_like(upcarry_ref)

    u = u_ref[...]
    ext = jnp.concatenate([ucarry_ref[...], u], axis=0)
    ucarry_ref[...] = u[tm - POOL_CARRY:, :]
    row = lax.broadcasted_iota(I32, (tm, 1), 0)
    n_seen = t * tm + row + 1
    outs = []
    for g, w in enumerate(POOL_WINDOWS):
        s = ext[:, g * POOL_GROUP:(g + 1) * POOL_GROUP]
        sh = 1
        while sh < w:
            s = s + pltpu.roll(s, sh, 0)
            sh *= 2
        cnt = jnp.minimum(w, n_seen).astype(F32)
        pooled = s[POOL_CARRY:, :] / cnt - u[:, g * POOL_GROUP:(g + 1) * POOL_GROUP]
        outs.append(jnp.dot(pooled.astype(BF16), wpool_ref[g], preferred_element_type=F32))
    pool = jnp.concatenate(outs, axis=1) * pscale_ref[...]
    mix = jnp.concatenate([attn_ref[...], pool.astype(BF16)], axis=1)
    x1 = x_ref[...] + jnp.dot(mix, wout_ref[...], preferred_element_type=F32)
    h2_ref[...] = _rms(x1, fng_ref[...]).astype(BF16)
    x1_ref[...] = x1
    row8 = lax.broadcasted_iota(I32, (SUBLANES, 1), 0)

    def ff_cols(ag, c):
        c0 = ag * D_FF + c * FF_CHUNK
        return slice(c0, c0 + FF_CHUNK)

    def up_proj(c, slot):
        h2 = h2_ref[...]
        for ag in range(2):
            up_ref[slot, ag] = jnp.dot(h2, wup_ref[:, ff_cols(ag, c)], preferred_element_type=F32)

    def conv_gate(c, slot):
        halves = []
        for ag in range(2):
            up = up_ref[slot, ag]
            cols = ff_cols(ag, c)
            prev = upcarry_ref[:, cols]
            p1 = prev[SUBLANES - 1:SUBLANES, :]
            p2 = prev[SUBLANES - 2:SUBLANES - 1, :]
            r1 = pltpu.roll(up, 1, 0)
            r2 = pltpu.roll(up, 2, 0)
            top1 = jnp.where(row8 == 0, p1, r1[:SUBLANES])
            top2 = jnp.where(row8 == 0, p2, jnp.where(row8 == 1, p1, r2[:SUBLANES]))
            up1 = jnp.concatenate([top1, r1[SUBLANES:]], axis=0)
            up2 = jnp.concatenate([top2, r2[SUBLANES:]], axis=0)
            cw = cw_ref[:, cols]
            conv = cb_ref[:, cols] + cw[0:1, :] * up2 + cw[1:2, :] * up1 + cw[2:3, :] * up
            last = up[tm - SUBLANES:, :]
            upcarry_ref[:, cols] = last
            cs_ref[0, :, cols] = last
            halves.append(conv)
        gated_ref[c] = _silu_gate(halves[0], halves[1]).astype(BF16)

    up_proj(0, 0)
    for c in range(N_FF_CHUNKS - 1):
        up_proj(c + 1, (c + 1) % 2)
        conv_gate(c, c % 2)
    conv_gate(N_FF_CHUNKS - 1, (N_FF_CHUNKS - 1) % 2)
    gated = jnp.concatenate([gated_ref[c] for c in range(N_FF_CHUNKS)], axis=1)
    y_ref[...] = x1_ref[...] + jnp.dot(gated, wdown_ref[...], preferred_element_type=F32)


def _weight_specs(layer):
    def spec(shape):
        nd = len(shape)
        return pl.BlockSpec((None,) + shape, lambda *_: (layer,) + (0,) * nd, pipeline_mode=pl.Buffered(1))
    return [
        spec((len(POOL_WINDOWS), POOL_GROUP, POOL_GROUP)),
        spec((1, POOL_WIDTH)),
        spec((D_MODEL, D_MODEL)),
        spec((1, D_MODEL)),
        spec((D_MODEL, 2 * D_FF)),
        spec((SUBLANES, 2 * D_FF)),
        spec((1, 2 * D_FF)),
        spec((D_FF, D_MODEL)),
    ]


def _prompt_ffn(x, attn, u, layer, weights, n_batch, tm):
    n = x.shape[0]
    tiles = n // n_batch // tm
    row = lambda b, t: (b * tiles + t, 0)
    return pl.pallas_call(
        _prompt_ffn_kernel,
        grid=(n_batch, tiles),
        in_specs=[pl.BlockSpec((tm, D_MODEL), row), pl.BlockSpec((tm, ATTN_WIDTH), row),
                  pl.BlockSpec((tm, POOL_WIDTH), row)] + _weight_specs(layer),
        out_specs=[pl.BlockSpec((tm, D_MODEL), row),
                   pl.BlockSpec((1, SUBLANES, 2 * D_FF), lambda b, t: (b, 0, 0))],
        out_shape=[jax.ShapeDtypeStruct((n, D_MODEL), F32),
                   jax.ShapeDtypeStruct((n_batch, SUBLANES, 2 * D_FF), F32)],
        scratch_shapes=[pltpu.VMEM((POOL_CARRY, POOL_WIDTH), F32),
                        pltpu.VMEM((SUBLANES, 2 * D_FF), F32),
                        pltpu.VMEM((tm, D_MODEL), BF16),
                        pltpu.VMEM((tm, D_MODEL), F32),
                        pltpu.VMEM((2, 2, tm, FF_CHUNK), F32),
                        pltpu.VMEM((N_FF_CHUNKS, tm, FF_CHUNK), BF16)],
        compiler_params=pltpu.CompilerParams(dimension_semantics=("arbitrary", "arbitrary"),
                                             vmem_limit_bytes=VMEM_LIMIT),
        name="prompt_ffn",
    )(x, attn, u, *weights)


def _sample_ffn_kernel(x_ref, attn_ref, u_ref, pstate_ref, cstate_ref, wpool_ref, pscale_ref, wout_ref,
                       fng_ref, wup_ref, cw_ref, cb_ref, wdown_ref, y_ref, cs_ref, h2_ref, gated_ref,
                       *, n_seq):
    n = x_ref.shape[0]
    u = u_ref[...]
    ext = jnp.concatenate([pstate_ref[...], u], axis=0)
    sums = {1: ext}
    w = 1
    while w < max(POOL_WINDOWS):
        prev = sums[w]
        sums[2 * w] = prev[w * n_seq:, :] + prev[:prev.shape[0] - w * n_seq, :]
        w *= 2
    outs = []
    for g, w in enumerate(POOL_WINDOWS):
        r0 = (POOL_BUF + 1 - w) * n_seq
        win = sums[w][r0:r0 + n, g * POOL_GROUP:(g + 1) * POOL_GROUP]
        pooled = win / float(w) - u[:, g * POOL_GROUP:(g + 1) * POOL_GROUP]
        outs.append(jnp.dot(pooled.astype(BF16), wpool_ref[g], preferred_element_type=F32))
    pool = jnp.concatenate(outs, axis=1) * pscale_ref[...]
    mix = jnp.concatenate([attn_ref[...], pool.astype(BF16)], axis=1)
    x1 = x_ref[...] + jnp.dot(mix, wout_ref[...], preferred_element_type=F32)
    h2_ref[...] = _rms(x1, fng_ref[...]).astype(BF16)

    h2 = h2_ref[...]
    for c in range(N_FF_CHUNKS):
        halves = []
        for ag in range(2):
            c0 = ag * D_FF + c * FF_CHUNK
            cols = slice(c0, c0 + FF_CHUNK)
            up = jnp.dot(h2, wup_ref[:, cols], preferred_element_type=F32)
            ext_up = jnp.concatenate([cstate_ref[:, cols], up], axis=0)
            cw = cw_ref[:, cols]
            conv = cb_ref[:, cols]
            for j in range(CONV_WIDTH):
                conv = conv + cw[j:j + 1, :] * ext_up[j * n_seq:j * n_seq + n, :]
            cs_ref[:, cols] = ext_up[n:, :]
            halves.append(conv)
        gated_ref[c] = _silu_gate(halves[0], halves[1]).astype(BF16)
    gated = jnp.concatenate([gated_ref[c] for c in range(N_FF_CHUNKS)], axis=1)
    y_ref[...] = x1 + jnp.dot(gated, wdown_ref[...], preferred_element_type=F32)


def _sample_ffn(x, attn, u, pool_state, conv_state, layer, weights, n_seq):
    n = x.shape[0]
    n_prev = (CONV_WIDTH - 1) * n_seq
    full = lambda shape: pl.BlockSpec(shape, lambda i: (0,) * len(shape))
    lay = lambda shape: pl.BlockSpec((None,) + shape, lambda i: (layer,) + (0,) * len(shape))
    return pl.pallas_call(
        functools.partial(_sample_ffn_kernel, n_seq=n_seq),
        grid=(1,),
        in_specs=[full((n, D_MODEL)), full((n, ATTN_WIDTH)), full((n, POOL_WIDTH)),
                  lay((POOL_BUF * n_seq, POOL_WIDTH)),
                  lay((n_prev, 2 * D_FF))] + _weight_specs(layer),
        out_specs=[full((n, D_MODEL)), full((n_prev, 2 * D_FF))],
        out_shape=[jax.ShapeDtypeStruct((n, D_MODEL), F32),
                   jax.ShapeDtypeStruct((n_prev, 2 * D_FF), F32)],
        scratch_shapes=[pltpu.VMEM((n, D_MODEL), BF16), pltpu.VMEM((N_FF_CHUNKS, n, FF_CHUNK), BF16)],
        compiler_params=pltpu.CompilerParams(dimension_semantics=("arbitrary",), vmem_limit_bytes=VMEM_LIMIT),
        name="sample_ffn",
    )(x, attn, u, pool_state, conv_state, *weights)


def kernel(x_prompt, x_sample, cache_k, cache_v, state_pool, state_conv, page_table, attn_norm, w_in, q_norm, k_norm, w_pool, pool_scale, w_out, ffn_norm, w_up, conv_w, conv_b, w_down):
    depth = w_in.shape[0]
    bp, t_p, _ = x_prompt.shape
    bs, t_s, _ = x_sample.shape
    n_phys = cache_k.shape[1]
    n_pages = page_table.shape[1]
    past_len = n_pages * PAGE_SIZE
    assert t_p % MOBA_BLOCK == 0 and past_len % MOBA_BLOCK == 0 and t_p >= POOL_CARRY
    assert t_s <= SUBLANES and 2 * (t_p // MOBA_BLOCK) <= HEAD_DIM

    w_qu = jnp.concatenate([w_in[:, :, :ATTN_WIDTH], w_in[:, :, 3 * ATTN_WIDTH:]], axis=2).astype(BF16)
    w_kv = w_in[:, :, ATTN_WIDTH:3 * ATTN_WIDTH].astype(BF16)
    q_gain = jnp.tile(q_norm, (1, N_HEADS))[:, None, :]
    k_gain = jnp.tile(k_norm, (1, N_HEADS))[:, None, :]
    k_gain_t = jnp.broadcast_to(k_norm[:, :, None], (depth, HEAD_DIM, LANES))
    in_weights = (attn_norm[:, None, :], w_qu, w_kv, q_gain)
    conv_w_rows = jnp.pad(conv_w, ((0, 0), (0, SUBLANES - CONV_WIDTH), (0, 0)))
    ffn_weights = (w_pool.astype(BF16), pool_scale[:, None, :], w_out.astype(BF16), ffn_norm[:, None, :],
                   w_up.astype(BF16), conv_w_rows, conv_b[:, None, :], w_down.astype(BF16))
    head_of = np.arange(2 * LANES) % LANES // HEAD_DIM
    gsum = jnp.asarray(head_of[:, None] == head_of[None, :LANES], BF16)
    slopes_np = np.exp2(-np.arange(1, N_HEADS + 1, dtype=np.float64)).astype(np.float32)
    slopes = jnp.asarray(slopes_np)
    slope_rows = jnp.asarray(np.tile(slopes_np, t_s)[:, None])

    cache_kt = jnp.transpose(cache_k, (0, 1, 3, 4, 2)).reshape(depth * n_phys, ATTN_WIDTH, PAGE_SIZE)
    cache_vt = jnp.transpose(cache_v, (0, 1, 3, 4, 2)).reshape(depth * n_phys, ATTN_WIDTH, PAGE_SIZE)
    pool_state_tm = jnp.transpose(state_pool, (0, 2, 1, 3)).reshape(depth, POOL_BUF * bs, POOL_WIDTH)
    conv_state_tm = jnp.transpose(state_conv, (0, 2, 1, 3)).reshape(depth, (CONV_WIDTH - 1) * bs, 2 * D_FF)

    n_p = bp * t_p
    tm_p = 512 if t_p % 512 == 0 else MOBA_BLOCK
    tm_in = 2 * tm_p if t_p % (2 * tm_p) == 0 else tm_p
    y_p = x_prompt.reshape(n_p, D_MODEL)
    y_s = jnp.transpose(x_sample, (1, 0, 2)).reshape(t_s * bs, D_MODEL)
    pad_new = 2 * SUBLANES - t_s
    page_ids = page_table[None] + (jnp.arange(depth, dtype=I32) * n_phys)[:, None, None]
    kv_all = [jnp.zeros((depth, bp, ATTN_WIDTH, t_p), F32) for _ in range(2)]

    outs = [[] for _ in range(6)]
    for l in range(depth):
        q, u, *kv_all = _inproj_prompt(y_p, l, bp, in_weights, k_gain_t, gsum, tm_in, kv_all)
        attn = _prompt_attn(slopes, q.reshape(bp, t_p, ATTN_WIDTH), kv_all[0], kv_all[1], l)
        y_p, cs_p = _prompt_ffn(y_p, attn.reshape(n_p, ATTN_WIDTH), u, l, ffn_weights, bp, tm_p)
        outs[0].append(u.reshape(bp, t_p, POOL_WIDTH)[:, t_p - POOL_BUF:, :])
        outs[1].append(cs_p[:, SUBLANES - (CONV_WIDTH - 1):, :])

        qs, ks, vs, us = _inproj_rows(y_s, l, in_weights, k_gain, gsum)
        to_seq = lambda a: jnp.transpose(a.reshape(t_s, bs, ATTN_WIDTH), (1, 0, 2))
        qs_b, ks_b, vs_b, us_b = to_seq(qs), to_seq(ks), to_seq(vs), to_seq(us)
        q_rep = jnp.repeat(qs_b, N_HEADS, axis=1)
        k_new = jnp.pad(ks_b, ((0, 0), (0, pad_new), (0, 0)))
        v_new = jnp.pad(vs_b, ((0, 0), (0, pad_new), (0, 0)))
        attn_s = _decode_attn(page_ids[l], slope_rows, q_rep, k_new, v_new,
                              cache_kt, cache_vt, past_len, t_s)
        attn_s = jnp.transpose(attn_s, (1, 0, 2)).reshape(t_s * bs, ATTN_WIDTH).astype(BF16)
        y_s, cs_s = _sample_ffn(y_s, attn_s, us, pool_state_tm, conv_state_tm, l, ffn_weights, bs)
        outs[2].append(ks_b.reshape(bs, t_s, N_HEADS, HEAD_DIM))
        outs[3].append(vs_b.reshape(bs, t_s, N_HEADS, HEAD_DIM))
        outs[4].append(jnp.concatenate([state_pool[l], us_b], axis=1)[:, t_s:, :])
        outs[5].append(jnp.transpose(cs_s.reshape(CONV_WIDTH - 1, bs, 2 * D_FF), (1, 0, 2)))

    y_prompt = y_p.reshape(bp, t_p, D_MODEL)
    y_sample = jnp.transpose(y_s.reshape(t_s, bs, D_MODEL), (1, 0, 2))
    new_k_p, new_v_p = (jnp.transpose(a.reshape(depth, bp, N_HEADS, HEAD_DIM, t_p), (0, 1, 4, 2, 3))
                        for a in kv_all)
    stacked = [jnp.stack(o) for o in outs]
    return (y_prompt, y_sample, new_k_p, new_v_p) + tuple(stacked)
```
